```python
import math
import jax, jax.numpy as jnp
from jax import lax
import numpy as np

D_MODEL = 2048
BATCH = 2
SEQ = 8192
DEPTH = 1

SSM_EXPAND = 2
D_SSM = SSM_EXPAND * D_MODEL
SSM_HEAD_DIM = 64
N_SSM_HEADS = D_SSM // SSM_HEAD_DIM
N_GROUPS = 8
HEADS_PER_GROUP = N_SSM_HEADS // N_GROUPS
D_STATE = 128
D_XBC = D_SSM + 2 * N_GROUPS * D_STATE
SSM_CONV = 5
CHUNK = 128
D_CONV = D_MODEL
CONV_WIDTH = 31
D_FF = -(-8 * D_MODEL // 768) * 256
N_MOD = 6
EPS = 1e-6
IN_SPLITS = (D_SSM,
             D_SSM + D_XBC,
             D_SSM + D_XBC + 2 * N_SSM_HEADS,
             D_SSM + D_XBC + 2 * N_SSM_HEADS + 2 * D_CONV)
IN_COLS = IN_SPLITS[-1] + 2 * D_MODEL

kernel_name = 'hybrid_bidir_ssd_conformer_gated_block'


def _rmsnorm(x, w):
    xf = x.astype(jnp.float32)
    y = xf * lax.rsqrt(jnp.mean(xf * xf, axis=-1, keepdims=True) + EPS)
    return (y * w.astype(jnp.float32)).astype(x.dtype)


def _layernorm(x, g, b):
    xf = x.astype(jnp.float32)
    xc = xf - jnp.mean(xf, axis=-1, keepdims=True)
    y = xc * lax.rsqrt(jnp.mean(xc * xc, axis=-1, keepdims=True) + EPS)
    return (y * g.astype(jnp.float32) + b.astype(jnp.float32)).astype(x.dtype)


def _dwconv(u, w, bias):
    k = w.shape[0]
    pad = (k - 1) // 2
    out = lax.conv_general_dilated(
        u, w[:, None, :].astype(u.dtype), window_strides=(1,), padding=[(pad, pad)],
        dimension_numbers=('NWC', 'WIO', 'NWC'), feature_group_count=u.shape[-1])
    return out + bias


def _flip(t):
    return jnp.flip(t, axis=1)


def _segsum(a):
    t = a.shape[-1]
    cs = jnp.cumsum(a, axis=-1)
    diff = cs[..., :, None] - cs[..., None, :]
    return jnp.where(jnp.tril(jnp.ones((t, t), dtype=bool)), diff, -jnp.inf)


def _ssd(xdt, adt, bm, cm):
    bsz, s, g, j, p = xdt.shape
    n = bm.shape[-1]
    nc = s // CHUNK
    x = xdt.reshape(bsz, nc, CHUNK, g, j, p)
    b = bm.reshape(bsz, nc, CHUNK, g, n)
    c = cm.reshape(bsz, nc, CHUNK, g, n)
    a = jnp.moveaxis(adt.reshape(bsz, nc, CHUNK, g, j), (1, 2), (3, 4))
    a_cs = jnp.cumsum(a, axis=-1)
    cb = jnp.einsum('bzlgn,bzsgn->bgzls', c, b)
    m = cb[:, :, None] * jnp.exp(_segsum(a))
    y_diag = jnp.einsum('bgjzls,bzsgjp->bzlgjp', m, x)
    decay_to_end = jnp.moveaxis(jnp.exp(a_cs[..., -1:] - a_cs), (3, 4), (1, 2))
    states = jnp.einsum('bzlgn,bzlgjp->bzgjpn', b, x * decay_to_end[..., None])
    chunk_decay = jnp.exp(a_cs[..., -1])

    def step(h, inp):
        s_z, d_z = inp
        return h * d_z[..., None, None] + s_z, h

    h0 = jnp.zeros((bsz, g, j, p, n), x.dtype)
    _, h_in = lax.scan(step, h0, (jnp.moveaxis(states, 1, 0), jnp.moveaxis(chunk_decay, -1, 0)))
    decay_in = jnp.moveaxis(jnp.exp(a_cs), (3, 4), (1, 2))
    y_off = jnp.einsum('bzlgn,zbgjpn->bzlgjp', c, h_in) * decay_in[..., None]
    return (y_diag + y_off).reshape(bsz, s, g, j, p)


def _ssd_branch(z, xbc, dt_raw, w_conv_ssm, b_conv_ssm, dt_bias_fwd, dt_bias_bwd,
                a_log_fwd, a_log_bwd, d_skip, g_ssm_norm, w_ssm_out):
    f32 = jnp.float32
    bsz, s, _ = xbc.shape
    xbc = jax.nn.silu(_dwconv(xbc, w_conv_ssm, b_conv_ssm))
    xs, bm, cm = jnp.split(xbc.astype(f32), (D_SSM, D_SSM + N_GROUPS * D_STATE), axis=-1)
    xh = xs.reshape(bsz, s, N_GROUPS, HEADS_PER_GROUP, SSM_HEAD_DIM)
    bm = bm.reshape(bsz, s, N_GROUPS, D_STATE)
    cm = cm.reshape(bsz, s, N_GROUPS, D_STATE)
    dtf_raw, dtb_raw = jnp.split(dt_raw.astype(f32), 2, axis=-1)
    hshape = (bsz, s, N_GROUPS, HEADS_PER_GROUP)
    dt_f = jax.nn.softplus(dtf_raw + dt_bias_fwd.astype(f32)).reshape(hshape)
    dt_b = jax.nn.softplus(dtb_raw + dt_bias_bwd.astype(f32)).reshape(hshape)
    a_f = -jnp.exp(a_log_fwd.astype(f32)).reshape(N_GROUPS, HEADS_PER_GROUP)
    a_b = -jnp.exp(a_log_bwd.astype(f32)).reshape(N_GROUPS, HEADS_PER_GROUP)
    y_fwd = _ssd(xh * dt_f[..., None], dt_f * a_f, bm, cm)
    y_bwd = _flip(_ssd(_flip(xh * dt_b[..., None]), _flip(dt_b * a_b), _flip(bm), _flip(cm)))
    y = y_fwd + y_bwd + d_skip.astype(f32).reshape(N_GROUPS, HEADS_PER_GROUP, 1) * xh
    gshape = (bsz, s, N_GROUPS, D_SSM // N_GROUPS)
    y = y.reshape(gshape) * jax.nn.silu(z.astype(f32)).reshape(gshape)
    y = y * lax.rsqrt(jnp.mean(y * y, axis=-1, keepdims=True) + EPS)
    y = y * g_ssm_norm.astype(f32).reshape(N_GROUPS, D_SSM // N_GROUPS)
    return y.reshape(bsz, s, D_SSM).astype(z.dtype) @ w_ssm_out


def _conformer_branch(glu_in, b_glu, w_dw, b_dw, ln_g, ln_b, w_conv_out, b_conv_out):
    u = jax.nn.glu(glu_in + b_glu, axis=-1)
    u = _dwconv(u, w_dw, b_dw)
    u = jax.nn.silu(_layernorm(u, ln_g, ln_b))
    return u @ w_conv_out + b_conv_out


def _mixer(h, w_in, w_conv_ssm, b_conv_ssm, dt_bias_fwd, dt_bias_bwd, a_log_fwd, a_log_bwd,
           d_skip, g_ssm_norm, w_ssm_out, b_glu, w_dw, b_dw, ln_g, ln_b, w_conv_out,
           b_conv_out, b_gate, w_mix_out):
    proj = h @ w_in
    z, xbc, dt_raw, glu_in, gate_logits = jnp.split(proj, IN_SPLITS, axis=-1)
    y_a = _ssd_branch(z, xbc, dt_raw, w_conv_ssm, b_conv_ssm, dt_bias_fwd, dt_bias_bwd,
                      a_log_fwd, a_log_bwd, d_skip, g_ssm_norm, w_ssm_out)
    y_b = _conformer_branch(glu_in, b_glu, w_dw, b_dw, ln_g, ln_b, w_conv_out, b_conv_out)
    g_a, g_b = jnp.split(jax.nn.sigmoid(gate_logits + b_gate), 2, axis=-1)
    return (g_a * y_a + g_b * y_b) @ w_mix_out


def _ffn(h, w_gate_up, w_down):
    gt, up = jnp.split(h @ w_gate_up, 2, axis=-1)
    return (jax.nn.silu(gt) * up) @ w_down


def setup_inputs(seed: int = 0) -> dict:
    key = jax.random.key(seed)
    ks = jax.random.split(key, 32)
    L = DEPTH
    f32 = jnp.float32

    def nrm(i, shape, scale):
        return scale * jax.random.normal(ks[i], shape, f32)

    def gain(i, shape):
        return 1.0 + 0.1 * jax.random.normal(ks[i], shape, f32)

    dt0 = jnp.exp(jax.random.uniform(ks[9], (2, L, N_SSM_HEADS), f32, math.log(1e-3), math.log(1e-1)))
    dt_bias = dt0 + jnp.log(-jnp.expm1(-dt0))
    a_log = jnp.log(jax.random.uniform(ks[10], (2, L, N_SSM_HEADS), f32, 1.0, 16.0))
    return {
        'x': nrm(0, (BATCH, SEQ, D_MODEL), 1.0),
        'c': nrm(1, (BATCH, D_MODEL), 1.0),
        'w_ada': nrm(2, (L, D_MODEL, N_MOD * D_MODEL), 0.5 * D_MODEL ** -0.5),
        'b_ada': nrm(3, (L, N_MOD * D_MODEL), 0.02),
        'g_pre_mix': gain(4, (L, D_MODEL)),
        'g_post_mix': gain(5, (L, D_MODEL)),
        'w_in': nrm(6, (L, D_MODEL, IN_COLS), D_MODEL ** -0.5),
        'w_conv_ssm': nrm(7, (L, SSM_CONV, D_XBC), SSM_CONV ** -0.5),
        'b_conv_ssm': nrm(8, (L, D_XBC), 0.02),
        'dt_bias_fwd': dt_bias[0],
        'dt_bias_bwd': dt_bias[1],
        'a_log_fwd': a_log[0],
        'a_log_bwd': a_log[1],
        'd_skip': gain(11, (L, N_SSM_HEADS)),
        'g_ssm_norm': gain(12, (L, D_SSM)),
        'w_ssm_out': nrm(13, (L, D_SSM, D_MODEL), D_SSM ** -0.5),
        'b_glu': nrm(14, (L, 2 * D_CONV), 0.02),
        'w_dw': nrm(15, (L, CONV_WIDTH, D_CONV), CONV_WIDTH ** -0.5),
        'b_dw': nrm(16, (L, D_CONV), 0.02),
        'ln_g': gain(17, (L, D_CONV)),
        'ln_b': nrm(18, (L, D_CONV), 0.02),
        'w_conv_out': nrm(19, (L, D_CONV, D_MODEL), D_CONV ** -0.5),
        'b_conv_out': nrm(20, (L, D_MODEL), 0.02),
        'b_gate': nrm(21, (L, 2 * D_MODEL), 0.02),
        'w_mix_out': nrm(22, (L, D_MODEL, D_MODEL), D_MODEL ** -0.5),
        'g_pre_ffn': gain(23, (L, D_MODEL)),
        'g_post_ffn': gain(24, (L, D_MODEL)),
        'w_gate_up': nrm(25, (L, D_MODEL, 2 * D_FF), D_MODEL ** -0.5),
        'w_down': nrm(26, (L, D_FF, D_MODEL), D_FF ** -0.5),
    }


def reference(x, c, w_ada, b_ada, g_pre_mix, g_post_mix, w_in, w_conv_ssm, b_conv_ssm,
              dt_bias_fwd, dt_bias_bwd, a_log_fwd, a_log_bwd, d_skip, g_ssm_norm, w_ssm_out,
              b_glu, w_dw, b_dw, ln_g, ln_b, w_conv_out, b_conv_out, b_gate, w_mix_out,
              g_pre_ffn, g_post_ffn, w_gate_up, w_down):
    c_act = jax.nn.silu(c)
    for l in range(DEPTH):
        mod = (c_act @ w_ada[l] + b_ada[l])[:, None, :]
        sh1, sc1, g1, sh2, sc2, g2 = jnp.split(mod, N_MOD, axis=-1)
        h = _rmsnorm(x, g_pre_mix[l]) * (1 + sc1) + sh1
        mix = _mixer(h, w_in[l], w_conv_ssm[l], b_conv_ssm[l], dt_bias_fwd[l], dt_bias_bwd[l],
                     a_log_fwd[l], a_log_bwd[l], d_skip[l], g_ssm_norm[l], w_ssm_out[l],
                     b_glu[l], w_dw[l], b_dw[l], ln_g[l], ln_b[l], w_conv_out[l],
                     b_conv_out[l], b_gate[l], w_mix_out[l])
        x = x + g1 * _rmsnorm(mix, g_post_mix[l])
        h = _rmsnorm(x, g_pre_ffn[l]) * (1 + sc2) + sh2
        x = x + g2 * _rmsnorm(_ffn(h, w_gate_up[l], w_down[l]), g_post_ffn[l])
    return x
```

```python
import functools

import jax
import jax.numpy as jnp
from jax import lax
from jax.experimental import pallas as pl
from jax.experimental.pallas import tpu as pltpu

F32 = jnp.float32
BF16 = jnp.bfloat16
HIGHEST = lax.Precision.HIGHEST

EPS = 1e-6
N_GROUPS = 8
D_STATE = 128
CHUNK = 128
V7X_VMEM_LIMIT_BYTES = 56 * 1024 * 1024
SUBLANES = 8


def _cp(*sem):
    return pltpu.CompilerParams(dimension_semantics=sem, vmem_limit_bytes=V7X_VMEM_LIMIT_BYTES)


def _silu(v):
    return v * jax.nn.sigmoid(v)


def _softplus(v):
    return jnp.maximum(v, 0.0) + jnp.log1p(jnp.exp(-jnp.abs(v)))


def _bdot(a, b):
    return jnp.dot(a, b, preferred_element_type=F32)


def _row_tile(n, want):
    t = min(n, want)
    assert n % t == 0, (n, t)
    return t


def _ada_kernel(c_ref, w_ref, b_ref, o_ref):
    ca = _silu(c_ref[...]).astype(BF16)
    o_ref[...] = _bdot(ca, w_ref[...].astype(BF16)) + b_ref[...]


def _ada(c, w, b):
    bsz, d = c.shape
    n = w.shape[1]
    rows = 16
    tn = _row_tile(n, 1024)
    cpad = jnp.zeros((rows, d), F32).at[:bsz].set(c)
    out = pl.pallas_call(
        _ada_kernel,
        grid=(n // tn,),
        in_specs=[pl.BlockSpec((rows, d), lambda j: (0, 0)),
                  pl.BlockSpec((d, tn), lambda j: (0, j)),
                  pl.BlockSpec((1, tn), lambda j: (0, j))],
        out_specs=pl.BlockSpec((rows, tn), lambda j: (0, j)),
        out_shape=jax.ShapeDtypeStruct((rows, n), F32),
        compiler_params=_cp("arbitrary"),
        name="ada",
    )(cpad, w, b.reshape(1, n))
    return out[:bsz]


def _inproj_kernel(x_ref, sc_ref, sh_ref, g_ref, w_ref, wdt_ref, o_ref, odt_ref, h_ref):
    @pl.when(pl.program_id(1) == 0)
    def _():
        x = x_ref[...]
        y = x * lax.rsqrt(jnp.mean(x * x, axis=-1, keepdims=True) + EPS) * g_ref[...]
        h = (y * (1.0 + sc_ref[0]) + sh_ref[0]).astype(BF16)
        h_ref[...] = h
        odt_ref[...] = _bdot(h, wdt_ref[...])

    o_ref[...] = _bdot(h_ref[...], w_ref[...])


def _inproj(x2, sc, sh, g, w_main, w_dt, seq):
    t, d = x2.shape
    n = w_main.shape[1]
    ndt = w_dt.shape[1]
    tm = _row_tile(seq, 1024)
    tn = _row_tile(n, 1024)
    per_b = seq // tm
    return pl.pallas_call(
        _inproj_kernel,
        grid=(t // tm, n // tn),
        in_specs=[pl.BlockSpec((tm, d), lambda i, j: (i, 0)),
                  pl.BlockSpec((1, 1, d), lambda i, j: (i // per_b, 0, 0)),
                  pl.BlockSpec((1, 1, d), lambda i, j: (i // per_b, 0, 0)),
                  pl.BlockSpec((1, d), lambda i, j: (0, 0)),
                  pl.BlockSpec((d, tn), lambda i, j: (0, j)),
                  pl.BlockSpec((d, ndt), lambda i, j: (0, 0))],
        out_specs=[pl.BlockSpec((tm, tn), lambda i, j: (i, j)),
                   pl.BlockSpec((tm, ndt), lambda i, j: (i, 0))],
        out_shape=[jax.ShapeDtypeStruct((t, n), F32),
                   jax.ShapeDtypeStruct((t, ndt), F32)],
        scratch_shapes=[pltpu.VMEM((tm, d), BF16)],
        compiler_params=_cp("arbitrary", "arbitrary"),
        name="inproj",
    )(x2, sc, sh, g, w_main, w_dt)


def _fill_ext(ext_ref, prev, main, nxt, halo, ts):
    s = pl.program_id(1)
    last = pl.num_programs(1) - 1
    ext_ref[0:halo, :] = jnp.where(s > 0, prev, 0.0)
    ext_ref[halo:halo + ts, :] = main
    ext_ref[halo + ts:halo + ts + halo, :] = jnp.where(s < last, nxt, 0.0)


def _dwconv_rows(tap_rows, w_ref, b_ref, rb, lanes, taps):
    acc = jnp.broadcast_to(b_ref[:, lanes], (rb, lanes.stop - lanes.start))
    for k in range(taps):
        acc = acc + tap_rows(k) * w_ref[k:k + 1, lanes]
    return acc


def _halo_specs(ts, halo, seq, width, col_block):
    per = ts // halo
    nh = seq // halo

    def prev_map(b, s, *c):
        return (b, jnp.maximum(s * per - 1, 0), col_block(*c))

    def main_map(b, s, *c):
        return (b, s, col_block(*c))

    def next_map(b, s, *c):
        return (b, jnp.minimum((s + 1) * per, nh - 1), col_block(*c))

    return [pl.BlockSpec((1, halo, width), prev_map),
            pl.BlockSpec((1, ts, width), main_map),
            pl.BlockSpec((1, halo, width), next_map)]


def _ssmconv_kernel(taps, halo, ts, rb, prev_ref, main_ref, next_ref, w_ref, b_ref, o_ref, ext_ref):
    _fill_ext(ext_ref, prev_ref[0], main_ref[0], next_ref[0], halo, ts)
    lanes = slice(0, o_ref.shape[-1])
    first = halo - (taps - 1) // 2
    for r0 in range(0, ts, rb):
        acc = _dwconv_rows(lambda k: ext_ref[r0 + first + k:r0 + first + k + rb, lanes],
                           w_ref, b_ref, rb, lanes, taps)
        o_ref[0, r0:r0 + rb, :] = _silu(acc)


def _ssmconv(proj3, w, b, col_off, width):
    bsz, seq, _ = proj3.shape
    taps = w.shape[0]
    halo = 8
    ts = _row_tile(seq, 512)
    tc = _row_tile(width, 512)
    rb = 32
    off = col_off // tc
    specs = _halo_specs(ts, halo, seq, tc, lambda c: off + c)
    return pl.pallas_call(
        functools.partial(_ssmconv_kernel, taps, halo, ts, rb),
        grid=(bsz, seq // ts, width // tc),
        in_specs=specs + [pl.BlockSpec((taps, tc), lambda b_, s, c: (0, c)),
                          pl.BlockSpec((1, tc), lambda b_, s, c: (0, c))],
        out_specs=pl.BlockSpec((1, ts, tc), lambda b_, s, c: (b_, s, c)),
        out_shape=jax.ShapeDtypeStruct((bsz, seq, width), F32),
        scratch_shapes=[pltpu.VMEM((ts + 2 * halo, tc), F32)],
        compiler_params=_cp("arbitrary", "arbitrary", "arbitrary"),
        name="ssmconv",
    )(proj3, proj3, proj3, w, b.reshape(1, width))


def _ssd_kernel(reverse, n_heads, hpg, pdim, *refs):
    if reverse:
        (xs_ref, b_ref, c_ref, dt_ref, bias_ref, alog_ref, yf_ref, z_ref, gn_ref,
         o_ref, h_ref, y_ref) = refs
    else:
        (xs_ref, b_ref, c_ref, dt_ref, bias_ref, alog_ref, dskip_ref,
         o_ref, h_ref, y_ref) = refs
    g = pl.program_id(1)
    ln = CHUNK
    w = hpg * pdim
    hd2 = 2 * n_heads

    @pl.when(pl.program_id(2) == 0)
    def _():
        h_ref[...] = jnp.zeros_like(h_ref)

    ri = lax.broadcasted_iota(jnp.int32, (ln, ln), 0)
    ci = lax.broadcasted_iota(jnp.int32, (ln, ln), 1)
    mask = (ci >= ri) if reverse else (ci <= ri)
    cum_mat = mask.astype(F32)
    edge = 0 if reverse else ln - 1

    dt_all = _softplus(dt_ref[0] + bias_ref[...])
    a_all = dt_all * (-jnp.exp(alog_ref[...]))
    cs_all = jnp.dot(cum_mat, a_all, precision=HIGHEST, preferred_element_type=F32)

    base = (n_heads if reverse else 0) + g * hpg
    er = lax.broadcasted_iota(jnp.int32, (hd2, w), 0)
    eq = lax.broadcasted_iota(jnp.int32, (hd2, w), 1)
    expand = (er == base + eq // pdim).astype(F32)
    dt_x = jnp.dot(dt_all, expand, precision=HIGHEST, preferred_element_type=F32)
    cs_x = jnp.dot(cs_all, expand, precision=HIGHEST, preferred_element_type=F32)
    sr = lax.broadcasted_iota(jnp.int32, (8, hd2), 0)
    sq = lax.broadcasted_iota(jnp.int32, (8, hd2), 1)
    sel_t = (sq == base + sr).astype(F32)
    cs_t = lax.dot_general(sel_t, cs_all, (((1,), (1,)), ((), ())),
                           precision=HIGHEST, preferred_element_type=F32)

    cs_edge = cs_x[edge:edge + 1, :]
    xs = xs_ref[0]
    xdt = xs * dt_x
    bb = b_ref[0].astype(BF16)
    cb_ = c_ref[0].astype(BF16)
    bt = b_ref[0].T.astype(BF16)
    cb = lax.dot_general(cb_, bb, (((1,), (1,)), ((), ())), preferred_element_type=F32)

    h_in = h_ref[...]
    y_ref[...] = _bdot(cb_, h_in.astype(BF16)) * jnp.exp(cs_x)
    st = _bdot(bt, (xdt * jnp.exp(cs_edge - cs_x)).astype(BF16))
    h_ref[...] = h_in * jnp.exp(cs_edge) + st

    xdt_b = xdt.astype(BF16)
    for j in range(hpg):
        lanes = slice(j * pdim, (j + 1) * pdim)
        seg = cs_x[:, j * pdim:j * pdim + 1] - cs_t[j:j + 1, :]
        m = (cb * jnp.where(mask, jnp.exp(seg), 0.0)).astype(BF16)
        y_ref[:, lanes] += _bdot(m, xdt_b[:, lanes])

    if reverse:
        y = (y_ref[...] + yf_ref[0]) * _silu(z_ref[0])
        y = y * lax.rsqrt(jnp.mean(y * y, axis=-1, keepdims=True) + EPS) * gn_ref[...]
        o_ref[0] = y.astype(o_ref.dtype)
    else:
        o_ref[0] = y_ref[...] + dskip_ref[...] * xs


def _ssd(reverse, xbc3, dt3, bias_row, alog_row, extra, n_heads, d_ssm):
    bsz, seq, _ = xbc3.shape
    hd2 = 2 * n_heads
    hpg = n_heads // N_GROUPS
    pdim = d_ssm // n_heads
    w = hpg * pdim
    nc = seq // CHUNK
    ln = CHUNK

    def zi(z):
        return nc - 1 - z if reverse else z

    boff = d_ssm // D_STATE
    coff = (d_ssm + N_GROUPS * D_STATE) // D_STATE
    in_specs = [pl.BlockSpec((1, ln, w), lambda b, g, z: (b, zi(z), g)),
                pl.BlockSpec((1, ln, D_STATE), lambda b, g, z: (b, zi(z), boff + g)),
                pl.BlockSpec((1, ln, D_STATE), lambda b, g, z: (b, zi(z), coff + g)),
                pl.BlockSpec((1, ln, hd2), lambda b, g, z: (b, zi(z), 0)),
                pl.BlockSpec((1, hd2), lambda b, g, z: (0, 0)),
                pl.BlockSpec((1, hd2), lambda b, g, z: (0, 0))]
    args = [xbc3, xbc3, xbc3, dt3, bias_row, alog_row]
    if reverse:
        y_f, proj3, gnorm = extra
        in_specs += [pl.BlockSpec((1, ln, w), lambda b, g, z: (b, zi(z), g)),
                     pl.BlockSpec((1, ln, w), lambda b, g, z: (b, zi(z), g)),
                     pl.BlockSpec((1, w), lambda b, g, z: (0, g))]
        args += [y_f, proj3, gnorm]
        out_dtype = BF16
    else:
        (dskip,) = extra
        in_specs += [pl.BlockSpec((1, w), lambda b, g, z: (0, g))]
        args += [dskip]
        out_dtype = F32
    return pl.pallas_call(
        functools.partial(_ssd_kernel, reverse, n_heads, hpg, pdim),
        grid=(bsz, N_GROUPS, nc),
        in_specs=in_specs,
        out_specs=pl.BlockSpec((1, ln, w), lambda b, g, z: (b, zi(z), g)),
        out_shape=jax.ShapeDtypeStruct((bsz, seq, d_ssm), out_dtype),
        scratch_shapes=[pltpu.VMEM((D_STATE, w), F32), pltpu.VMEM((ln, w), F32)],
        compiler_params=_cp("arbitrary", "arbitrary", "arbitrary"),
        name="ssd_bwd" if reverse else "ssd_fwd",
    )(*args)


def _conf_kernel(taps, halo, ts, rb, lc, pa_ref, ma_ref, na_ref, pb_ref, mb_ref, nb_ref,
                 bga_ref, bgb_ref, w_ref, bdw_ref, lng_ref, lnb_ref, o_ref, ext_ref, shift_ref, conv_ref):
    d = o_ref.shape[-1]

    def glu(a, b):
        return (a + bga_ref[...]) * jax.nn.sigmoid(b + bgb_ref[...])

    _fill_ext(ext_ref, glu(pa_ref[0], pb_ref[0]), glu(ma_ref[0], mb_ref[0]),
              glu(na_ref[0], nb_ref[0]), halo, ts)
    nrows = shift_ref.shape[1]
    for q in range(SUBLANES):
        shift_ref[q] = ext_ref[q:q + nrows, :]
    first = halo - (taps - 1) // 2

    def conv_body(r, carry):
        r0 = pl.multiple_of(r * rb, rb)
        for c0 in range(0, d, lc):
            lanes = slice(c0, c0 + lc)

            def tap_rows(k):
                m, q = divmod(first + k, SUBLANES)
                return shift_ref[q, pl.ds(r0 + SUBLANES * m, rb), lanes]

            conv_ref[pl.ds(r0, rb), lanes] = _dwconv_rows(tap_rows, w_ref, bdw_ref, rb, lanes, taps)
        return carry

    lax.fori_loop(0, ts // rb, conv_body, 0)

    def ln_body(r, carry):
        r0 = pl.multiple_of(r * rb, rb)
        u = conv_ref[pl.ds(r0, rb), :]
        uc = u - jnp.mean(u, axis=-1, keepdims=True)
        y = uc * lax.rsqrt(jnp.mean(uc * uc, axis=-1, keepdims=True) + EPS)
        y = y * lng_ref[...] + lnb_ref[...]
        o_ref[0, pl.ds(r0, rb), :] = _silu(y).astype(o_ref.dtype)
        return carry

    lax.fori_loop(0, ts // rb, ln_body, 0)


def _conf(proj3, col_off, b_glu, w_dw, b_dw, ln_g, ln_b):
    bsz, seq, _ = proj3.shape
    taps, d = w_dw.shape
    halo = 16
    ts = _row_tile(seq, 256)
    rb = 32
    lc = _row_tile(d, 512)
    ablk = col_off // d
    specs_a = _halo_specs(ts, halo, seq, d, lambda: ablk)
    specs_b = _halo_specs(ts, halo, seq, d, lambda: ablk + 1)
    row = lambda b_, s: (0, 0)
    return pl.pallas_call(
        functools.partial(_conf_kernel, taps, halo, ts, rb, lc),
        grid=(bsz, seq // ts),
        in_specs=specs_a + specs_b + [pl.BlockSpec((1, d), row), pl.BlockSpec((1, d), row),
                                      pl.BlockSpec((taps, d), row), pl.BlockSpec((1, d), row),
                                      pl.BlockSpec((1, d), row), pl.BlockSpec((1, d), row)],
        out_specs=pl.BlockSpec((1, ts, d), lambda b_, s: (b_, s, 0)),
        out_shape=jax.ShapeDtypeStruct((bsz, seq, d), BF16),
        scratch_shapes=[pltpu.VMEM((ts + 2 * halo, d), F32),
                        pltpu.VMEM((SUBLANES, ts + 2 * halo - SUBLANES, d), F32),
                        pltpu.VMEM((ts, d), F32)],
        compiler_params=_cp("arbitrary", "arbitrary"),
        name="conf",
    )(proj3, proj3, proj3, proj3, proj3, proj3,
      b_glu[:d].reshape(1, d), b_glu[d:].reshape(1, d), w_dw, b_dw.reshape(1, d),
      ln_g.reshape(1, d), ln_b.reshape(1, d))


def _merge_kernel(ya_ref, u_ref, wa_ref, wb_ref, bb_ref, ga_ref, gb_ref, bga_ref, bgb_ref, o_ref):
    y_a = _bdot(ya_ref[...], wa_ref[...])
    y_b = _bdot(u_ref[...], wb_ref[...]) + bb_ref[...]
    g_a = jax.nn.sigmoid(ga_ref[...] + bga_ref[...])
    g_b = jax.nn.sigmoid(gb_ref[...] + bgb_ref[...])
    o_ref[...] = (g_a * y_a + g_b * y_b).astype(o_ref.dtype)


def _merge(y_ssd, u, w_ssm_out, w_conv_out, b_conv_out, proj2, gate_off, b_gate):
    t, d_ssm = y_ssd.shape
    d_conv = u.shape[1]
    d = w_ssm_out.shape[1]
    tm = _row_tile(t, 512)
    tn = _row_tile(d, 512)
    ga = gate_off // tn
    gb = (gate_off + d) // tn
    bg = b_gate.reshape(1, 2 * d)
    return pl.pallas_call(
        _merge_kernel,
        grid=(t // tm, d // tn),
        in_specs=[pl.BlockSpec((tm, d_ssm), lambda i, j: (i, 0)),
                  pl.BlockSpec((tm, d_conv), lambda i, j: (i, 0)),
                  pl.BlockSpec((d_ssm, tn), lambda i, j: (0, j)),
                  pl.BlockSpec((d_conv, tn), lambda i, j: (0, j)),
                  pl.BlockSpec((1, tn), lambda i, j: (0, j)),
                  pl.BlockSpec((tm, tn), lambda i, j: (i, ga + j)),
                  pl.BlockSpec((tm, tn), lambda i, j: (i, gb + j)),
                  pl.BlockSpec((1, tn), lambda i, j: (0, j)),
                  pl.BlockSpec((1, tn), lambda i, j: (0, d // tn + j))],
        out_specs=pl.BlockSpec((tm, tn), lambda i, j: (i, j)),
        out_shape=jax.ShapeDtypeStruct((t, d), BF16),
        compiler_params=_cp("arbitrary", "arbitrary"),
        name="merge",
    )(y_ssd, u, w_ssm_out, w_conv_out, b_conv_out.reshape(1, d), proj2, proj2, bg, bg)


def _mixout_kernel(m_ref, w_ref, x_ref, gpost_ref, g1_ref, gpre_ref, sc2_ref, sh2_ref, x1_ref, h2_ref):
    mix = _bdot(m_ref[...], w_ref[...])
    nm = mix * lax.rsqrt(jnp.mean(mix * mix, axis=-1, keepdims=True) + EPS) * gpost_ref[...]
    x1 = x_ref[...] + g1_ref[0] * nm
    x1_ref[...] = x1
    y = x1 * lax.rsqrt(jnp.mean(x1 * x1, axis=-1, keepdims=True) + EPS) * gpre_ref[...]
    h2_ref[...] = (y * (1.0 + sc2_ref[0]) + sh2_ref[0]).astype(h2_ref.dtype)


def _mixout(mix_in, w_mix_out, x2, g_post, g1, g_pre_ffn, sc2, sh2, seq):
    t, d = x2.shape
    tm = _row_tile(seq, 512)
    per_b = seq // tm
    row = lambda i: (0, 0)
    brow = lambda i: (i // per_b, 0, 0)
    return pl.pallas_call(
        _mixout_kernel,
        grid=(t // tm,),
        in_specs=[pl.BlockSpec((tm, d), lambda i: (i, 0)),
                  pl.BlockSpec((d, d), row),
                  pl.BlockSpec((tm, d), lambda i: (i, 0)),
                  pl.BlockSpec((1, d), row),
                  pl.BlockSpec((1, 1, d), brow),
                  pl.BlockSpec((1, d), row),
                  pl.BlockSpec((1, 1, d), brow),
                  pl.BlockSpec((1, 1, d), brow)],
        out_specs=[pl.BlockSpec((tm, d), lambda i: (i, 0)),
                   pl.BlockSpec((tm, d), lambda i: (i, 0))],
        out_shape=[jax.ShapeDtypeStruct((t, d), F32), jax.ShapeDtypeStruct((t, d), BF16)],
        compiler_params=_cp("arbitrary"),
        name="mixout",
    )(mix_in, w_mix_out, x2, g_post, g1, g_pre_ffn, sc2, sh2)


def _ffn_up_kernel(h_ref, wg_ref, wu_ref, o_ref):
    h = h_ref[...]
    o_ref[...] = (_silu(_bdot(h, wg_ref[...])) * _bdot(h, wu_ref[...])).astype(o_ref.dtype)


def _ffn_up(h2, w_gate_up):
    t, d = h2.shape
    d_ff = w_gate_up.shape[1] // 2
    tm = _row_tile(t, 1024)
    tn = _row_tile(d_ff, 512)
    nj = d_ff // tn
    return pl.pallas_call(
        _ffn_up_kernel,
        grid=(t // tm, nj),
        in_specs=[pl.BlockSpec((tm, d), lambda i, j: (i, 0)),
                  pl.BlockSpec((d, tn), lambda i, j: (0, j)),
                  pl.BlockSpec((d, tn), lambda i, j: (0, nj + j))],
        out_specs=pl.BlockSpec((tm, tn), lambda i, j: (i, j)),
        out_shape=jax.ShapeDtypeStruct((t, d_ff), BF16),
        compiler_params=_cp("arbitrary", "arbitrary"),
        name="ffn_up",
    )(h2, w_gate_up, w_gate_up)


def _ffn_down_kernel(a_ref, w_ref, x1_ref, gpost_ref, g2_ref, o_ref, acc_ref):
    k = pl.program_id(1)

    @pl.when(k == 0)
    def _():
        acc_ref[...] = jnp.zeros_like(acc_ref)

    acc_ref[...] += _bdot(a_ref[...], w_ref[...])

    @pl.when(k == pl.num_programs(1) - 1)
    def _():
        f = acc_ref[...]
        nf = f * lax.rsqrt(jnp.mean(f * f, axis=-1, keepdims=True) + EPS) * gpost_ref[...]
        o_ref[...] = x1_ref[...] + g2_ref[0] * nf


def _ffn_down(act, w_down, x1, g_post, g2, seq):
    t, d_ff = act.shape
    d = w_down.shape[1]
    tm = _row_tile(seq, 512)
    tk = _row_tile(d_ff, 512)
    per_b = seq // tm
    return pl.pallas_call(
        _ffn_down_kernel,
        grid=(t // tm, d_ff // tk),
        in_specs=[pl.BlockSpec((tm, tk), lambda i, k: (i, k)),
                  pl.BlockSpec((tk, d), lambda i, k: (k, 0)),
                  pl.BlockSpec((tm, d), lambda i, k: (i, 0)),
                  pl.BlockSpec((1, d), lambda i, k: (0, 0)),
                  pl.BlockSpec((1, 1, d), lambda i, k: (i // per_b, 0, 0))],
        out_specs=pl.BlockSpec((tm, d), lambda i, k: (i, 0)),
        out_shape=jax.ShapeDtypeStruct((t, d), F32),
        scratch_shapes=[pltpu.VMEM((tm, d), F32)],
        compiler_params=_cp("arbitrary", "arbitrary"),
        name="ffn_down",
    )(act, w_down, x1, g_post, g2)


def kernel(x, c, w_ada, b_ada, g_pre_mix, g_post_mix, w_in, w_conv_ssm, b_conv_ssm, dt_bias_fwd, dt_bias_bwd, a_log_fwd, a_log_bwd, d_skip, g_ssm_norm, w_ssm_out, b_glu, w_dw, b_dw, ln_g, ln_b, w_conv_out, b_conv_out, b_gate, w_mix_out, g_pre_ffn, g_post_ffn, w_gate_up, w_down):
    bsz, seq, d = x.shape
    depth = w_ada.shape[0]
    n_heads = dt_bias_fwd.shape[1]
    d_ssm = w_ssm_out.shape[1]
    d_xbc = w_conv_ssm.shape[2]
    d_conv = w_dw.shape[2]
    pdim = d_ssm // n_heads
    assert d_xbc == d_ssm + 2 * N_GROUPS * D_STATE and seq % CHUNK == 0 and n_heads % N_GROUPS == 0
    s1, s2, s3, s4 = d_ssm, d_ssm + d_xbc, d_ssm + d_xbc + 2 * n_heads, d_ssm + d_xbc + 2 * n_heads + 2 * d_conv
    xbc_off, glu_off, gate_off = d_ssm, d_ssm + d_xbc, d_ssm + d_xbc + 2 * d_conv

    x2 = x.reshape(bsz * seq, d)
    for l in range(depth):
        w_main = jnp.concatenate([w_in[l][:, :s2], w_in[l][:, s3:]], axis=1).astype(BF16)
        w_dt = w_in[l][:, s2:s3].astype(BF16)
        row = lambda v: v.reshape(1, -1)

        mod = _ada(c, w_ada[l], b_ada[l])
        sh1, sc1, g1, sh2, sc2, g2 = [m.reshape(bsz, 1, d) for m in jnp.split(mod, 6, axis=-1)]

        proj2, dt2 = _inproj(x2, sc1, sh1, row(g_pre_mix[l]), w_main, w_dt, seq)
        proj3 = proj2.reshape(bsz, seq, -1)
        dt3 = dt2.reshape(bsz, seq, -1)

        xbc3 = _ssmconv(proj3, w_conv_ssm[l], b_conv_ssm[l], xbc_off, d_xbc)
        bias_row = row(jnp.concatenate([dt_bias_fwd[l], dt_bias_bwd[l]]))
        alog_row = row(jnp.concatenate([a_log_fwd[l], a_log_bwd[l]]))
        dskip_row = row(jnp.repeat(d_skip[l], pdim))
        y_f = _ssd(False, xbc3, dt3, bias_row, alog_row, (dskip_row,), n_heads, d_ssm)
        y_ssd = _ssd(True, xbc3, dt3, bias_row, alog_row, (y_f, proj3, row(g_ssm_norm[l])), n_heads, d_ssm)

        u = _conf(proj3, glu_off, b_glu[l], w_dw[l], b_dw[l], ln_g[l], ln_b[l])

        mix_in = _merge(y_ssd.reshape(bsz * seq, d_ssm), u.reshape(bsz * seq, d_conv),
                        w_ssm_out[l].astype(BF16), w_conv_out[l].astype(BF16), b_conv_out[l],
                        proj2, gate_off, b_gate[l])
        x1, h2 = _mixout(mix_in, w_mix_out[l].astype(BF16), x2, row(g_post_mix[l]), g1,
                         row(g_pre_ffn[l]), sc2, sh2, seq)
        act = _ffn_up(h2, w_gate_up[l].astype(BF16))
        x2 = _ffn_down(act, w_down[l].astype(BF16), x1, row(g_post_ffn[l]), g2, seq)
    return x2.reshape(bsz, seq, d)
```

```python
import functools

import jax
import jax.numpy as jnp
from jax import lax
from jax.experimental import pallas as pl
from jax.experimental.pallas import tpu as pltpu

F32 = jnp.float32
BF16 = jnp.bfloat16
HIGHEST = lax.Precision.HIGHEST

EPS = 1e-6
N_GROUPS = 8
D_STATE = 128
CHUNK = 128
V7X_VMEM_LIMIT_BYTES = 56 * 1024 * 1024
SUBLANES = 8
LOG2E = 1.4426950408889634


def _cp(*sem):
    return pltpu.CompilerParams(dimension_semantics=sem, vmem_limit_bytes=V7X_VMEM_LIMIT_BYTES)


def _silu(v):
    return v * jax.nn.sigmoid(v)


def _softplus(v):
    return jnp.maximum(v, 0.0) + jnp.log1p(jnp.exp(-jnp.abs(v)))


def _bdot(a, b):
    return jnp.dot(a, b, preferred_element_type=F32)


def _row_tile(n, want):
    t = min(n, want)
    assert n % t == 0, (n, t)
    return t


def _ada_kernel(c_ref, w_ref, b_ref, o_ref):
    ca = _silu(c_ref[...]).astype(BF16)
    o_ref[...] = _bdot(ca, w_ref[...].astype(BF16)) + b_ref[...]


def _ada(c, w, b):
    bsz, d = c.shape
    n = w.shape[1]
    rows = 16
    tn = _row_tile(n, 1024)
    cpad = jnp.zeros((rows, d), F32).at[:bsz].set(c)
    out = pl.pallas_call(
        _ada_kernel,
        grid=(n // tn,),
        in_specs=[pl.BlockSpec((rows, d), lambda j: (0, 0)),
                  pl.BlockSpec((d, tn), lambda j: (0, j)),
                  pl.BlockSpec((1, tn), lambda j: (0, j))],
        out_specs=pl.BlockSpec((rows, tn), lambda j: (0, j)),
        out_shape=jax.ShapeDtypeStruct((rows, n), F32),
        compiler_params=_cp("arbitrary"),
        name="ada",
    )(cpad, w, b.reshape(1, n))
    return out[:bsz]


def _inproj_kernel(x_ref, sc_ref, sh_ref, g_ref, w_ref, wdt_ref, o_ref, odt_ref, h_ref):
    @pl.when(pl.program_id(1) == 0)
    def _():
        x = x_ref[...]
        y = x * lax.rsqrt(jnp.mean(x * x, axis=-1, keepdims=True) + EPS) * g_ref[...]
        h = (y * (1.0 + sc_ref[0]) + sh_ref[0]).astype(BF16)
        h_ref[...] = h
        odt_ref[...] = _bdot(h, wdt_ref[...])

    o_ref[...] = _bdot(h_ref[...], w_ref[...])


def _inproj(x2, sc, sh, g, w_main, w_dt, seq):
    t, d = x2.shape
    n = w_main.shape[1]
    ndt = w_dt.shape[1]
    tm = _row_tile(seq, 1024)
    tn = _row_tile(n, 1024)
    per_b = seq // tm
    return pl.pallas_call(
        _inproj_kernel,
        grid=(t // tm, n // tn),
        in_specs=[pl.BlockSpec((tm, d), lambda i, j: (i, 0)),
                  pl.BlockSpec((1, 1, d), lambda i, j: (i // per_b, 0, 0)),
                  pl.BlockSpec((1, 1, d), lambda i, j: (i // per_b, 0, 0)),
                  pl.BlockSpec((1, d), lambda i, j: (0, 0)),
                  pl.BlockSpec((d, tn), lambda i, j: (0, j)),
                  pl.BlockSpec((d, ndt), lambda i, j: (0, 0))],
        out_specs=[pl.BlockSpec((tm, tn), lambda i, j: (i, j)),
                   pl.BlockSpec((tm, ndt), lambda i, j: (i, 0))],
        out_shape=[jax.ShapeDtypeStruct((t, n), F32),
                   jax.ShapeDtypeStruct((t, ndt), F32)],
        scratch_shapes=[pltpu.VMEM((tm, d), BF16)],
        compiler_params=_cp("arbitrary", "arbitrary"),
        name="inproj",
    )(x2, sc, sh, g, w_main, w_dt)


def _fill_ext(ext_ref, prev, main, nxt, halo, ts):
    s = pl.program_id(1)
    last = pl.num_programs(1) - 1
    ext_ref[0:halo, :] = jnp.where(s > 0, prev, 0.0)
    ext_ref[halo:halo + ts, :] = main
    ext_ref[halo + ts:halo + ts + halo, :] = jnp.where(s < last, nxt, 0.0)


def _dwconv_rows(tap_rows, w_ref, b_ref, rb, lanes, taps):
    acc = jnp.broadcast_to(b_ref[:, lanes], (rb, lanes.stop - lanes.start))
    for k in range(taps):
        acc = acc + tap_rows(k) * w_ref[k:k + 1, lanes]
    return acc


def _halo_specs(ts, halo, seq, width, col_block):
    per = ts // halo
    nh = seq // halo

    def prev_map(b, s, *c):
        return (b, jnp.maximum(s * per - 1, 0), col_block(*c))

    def main_map(b, s, *c):
        return (b, s, col_block(*c))

    def next_map(b, s, *c):
        return (b, jnp.minimum((s + 1) * per, nh - 1), col_block(*c))

    return [pl.BlockSpec((1, halo, width), prev_map),
            pl.BlockSpec((1, ts, width), main_map),
            pl.BlockSpec((1, halo, width), next_map)]


def _ssmconv_kernel(taps, halo, ts, rb, prev_ref, main_ref, next_ref, w_ref, b_ref, o_ref, ext_ref):
    _fill_ext(ext_ref, prev_ref[0], main_ref[0], next_ref[0], halo, ts)
    lanes = slice(0, o_ref.shape[-1])
    first = halo - (taps - 1) // 2
    for r0 in range(0, ts, rb):
        acc = _dwconv_rows(lambda k: ext_ref[r0 + first + k:r0 + first + k + rb, lanes],
                           w_ref, b_ref, rb, lanes, taps)
        o_ref[0, r0:r0 + rb, :] = _silu(acc)


def _ssmconv(proj3, w, b, col_off, width):
    bsz, seq, _ = proj3.shape
    taps = w.shape[0]
    halo = 8
    ts = _row_tile(seq, 512)
    tc = _row_tile(width, 512)
    rb = 32
    off = col_off // tc
    specs = _halo_specs(ts, halo, seq, tc, lambda c: off + c)
    return pl.pallas_call(
        functools.partial(_ssmconv_kernel, taps, halo, ts, rb),
        grid=(bsz, seq // ts, width // tc),
        in_specs=specs + [pl.BlockSpec((taps, tc), lambda b_, s, c: (0, c)),
                          pl.BlockSpec((1, tc), lambda b_, s, c: (0, c))],
        out_specs=pl.BlockSpec((1, ts, tc), lambda b_, s, c: (b_, s, c)),
        out_shape=jax.ShapeDtypeStruct((bsz, seq, width), F32),
        scratch_shapes=[pltpu.VMEM((ts + 2 * halo, tc), F32)],
        compiler_params=_cp("arbitrary", "arbitrary", "arbitrary"),
        name="ssmconv",
    )(proj3, proj3, proj3, w, b.reshape(1, width))


def _ssd_kernel(reverse, n_heads, hpg, pdim, *refs):
    if reverse:
        (xs_ref, b_ref, c_ref, dt_ref, bias_ref, alog_ref, yf_ref, z_ref, gn_ref,
         o_ref, h_ref, dt_t, cs_t, din_t, wend_t, cdec_t) = refs
    else:
        (xs_ref, b_ref, c_ref, dt_ref, bias_ref, alog_ref, dskip_ref,
         o_ref, h_ref, dt_t, cs_t, din_t, wend_t, cdec_t) = refs
    ln = CHUNK
    w = hpg * pdim
    hd2 = 2 * n_heads
    nt = (((1,), (1,)), ((), ()))

    @pl.when(pl.program_id(1) == 0)
    def _():
        h_ref[...] = jnp.zeros_like(h_ref)

    ri = lax.broadcasted_iota(jnp.int32, (ln, ln), 0)
    ci = lax.broadcasted_iota(jnp.int32, (ln, ln), 1)
    mask = (ci >= ri) if reverse else (ci <= ri)
    cum = mask.astype(BF16)
    edge = 0 if reverse else ln - 1

    dt_all = _softplus(dt_ref[0] + bias_ref[...])
    a_all = dt_all * (-jnp.exp(alog_ref[...]) * LOG2E)
    a_hi = a_all.astype(BF16)
    r1 = a_all - a_hi.astype(F32)
    a_mid = r1.astype(BF16)
    a_lo = (r1 - a_mid.astype(F32)).astype(BF16)
    cs_all = _bdot(cum, a_hi) + _bdot(cum, a_mid) + _bdot(cum, a_lo)
    cs_edge = cs_all[edge:edge + 1, :]
    dt_t[...] = dt_all.T
    cs_t[...] = cs_all.T
    din_t[...] = jnp.exp2(cs_all).T
    wend_t[...] = (dt_all * jnp.exp2(cs_edge - cs_all)).T
    cdec_t[...] = jnp.broadcast_to(jnp.exp2(cs_edge), (ln, hd2)).T

    dirbase = n_heads if reverse else 0
    for g in range(N_GROUPS):
        lanes = slice(g * w, (g + 1) * w)
        xs = xs_ref[0, :, lanes]
        x_t = xs.T
        bgb = b_ref[0, :, g * D_STATE:(g + 1) * D_STATE].astype(BF16)
        cgb = c_ref[0, :, g * D_STATE:(g + 1) * D_STATE].astype(BF16)
        cb = lax.dot_general(cgb, bgb, nt, preferred_element_type=F32)
        h_in = h_ref[g]
        yoff_t = lax.dot_general(h_in.astype(BF16), cgb, nt, preferred_element_type=F32)
        y_parts, xw_parts, cdec_parts = [], [], []
        for j in range(hpg):
            c = dirbase + g * hpg + j
            rows = slice(j * pdim, (j + 1) * pdim)
            xdt_t = (x_t[rows] * dt_t[c:c + 1, :]).astype(BF16)
            xw_parts.append((x_t[rows] * wend_t[c:c + 1, :]).astype(BF16))
            seg = cs_all[:, c:c + 1] - cs_t[c:c + 1, :]
            m = (cb * jnp.where(mask, jnp.exp2(seg), 0.0)).astype(BF16)
            yd_t = lax.dot_general(xdt_t, m, nt, preferred_element_type=F32)
            y_parts.append(yd_t + yoff_t[rows] * din_t[c:c + 1, :])
            cdec_parts.append(jnp.broadcast_to(cdec_t[c:c + 1, :], (pdim, D_STATE)))
        y = jnp.concatenate(y_parts, axis=0).T
        st = _bdot(jnp.concatenate(xw_parts, axis=0), bgb)
        h_ref[g] = h_in * jnp.concatenate(cdec_parts, axis=0) + st
        if reverse:
            y = (y + yf_ref[0, :, lanes]) * _silu(z_ref[0, :, lanes])
            y = y * lax.rsqrt(jnp.mean(y * y, axis=-1, keepdims=True) + EPS) * gn_ref[:, lanes]
            o_ref[0, :, lanes] = y.astype(o_ref.dtype)
        else:
            o_ref[0, :, lanes] = y + dskip_ref[:, lanes] * xs


def _ssd(reverse, xbc3, dt3, bias_row, alog_row, extra, n_heads, d_ssm):
    bsz, seq, _ = xbc3.shape
    hd2 = 2 * n_heads
    hpg = n_heads // N_GROUPS
    pdim = d_ssm // n_heads
    w = hpg * pdim
    nc = seq // CHUNK
    ln = CHUNK
    gn = N_GROUPS * D_STATE
    assert CHUNK == D_STATE and d_ssm % gn == 0

    def chunk(b, z):
        return (b, nc - 1 - z if reverse else z)

    in_specs = [pl.BlockSpec((1, ln, d_ssm), lambda b, z: (*chunk(b, z), 0)),
                pl.BlockSpec((1, ln, gn), lambda b, z: (*chunk(b, z), d_ssm // gn)),
                pl.BlockSpec((1, ln, gn), lambda b, z: (*chunk(b, z), d_ssm // gn + 1)),
                pl.BlockSpec((1, ln, hd2), lambda b, z: (*chunk(b, z), 0)),
                pl.BlockSpec((1, hd2), lambda b, z: (0, 0)),
                pl.BlockSpec((1, hd2), lambda b, z: (0, 0))]
    args = [xbc3, xbc3, xbc3, dt3, bias_row, alog_row]
    if reverse:
        y_f, proj3, gnorm = extra
        in_specs += [pl.BlockSpec((1, ln, d_ssm), lambda b, z: (*chunk(b, z), 0)),
                     pl.BlockSpec((1, ln, d_ssm), lambda b, z: (*chunk(b, z), 0)),
                     pl.BlockSpec((1, d_ssm), lambda b, z: (0, 0))]
        args += [y_f, proj3, gnorm]
        out_dtype = BF16
    else:
        (dskip,) = extra
        in_specs += [pl.BlockSpec((1, d_ssm), lambda b, z: (0, 0))]
        args += [dskip]
        out_dtype = F32
    table = pltpu.VMEM((hd2, ln), F32)
    return pl.pallas_call(
        functools.partial(_ssd_kernel, reverse, n_heads, hpg, pdim),
        grid=(bsz, nc),
        in_specs=in_specs,
        out_specs=pl.BlockSpec((1, ln, d_ssm), lambda b, z: (*chunk(b, z), 0)),
        out_shape=jax.ShapeDtypeStruct((bsz, seq, d_ssm), out_dtype),
        scratch_shapes=[pltpu.VMEM((N_GROUPS, w, D_STATE), F32), table, table, table, table, table],
        compiler_params=_cp("arbitrary", "arbitrary"),
        name="ssd_bwd" if reverse else "ssd_fwd",
    )(*args)


def _conf_kernel(taps, halo, ts, rb, lc, pa_ref, ma_ref, na_ref, pb_ref, mb_ref, nb_ref,
                 bga_ref, bgb_ref, w_ref, bdw_ref, lng_ref, lnb_ref, o_ref, ext_ref, shift_ref, conv_ref):
    d = o_ref.shape[-1]

    def glu(a, b):
        return (a + bga_ref[...]) * jax.nn.sigmoid(b + bgb_ref[...])

    _fill_ext(ext_ref, glu(pa_ref[0], pb_ref[0]), glu(ma_ref[0], mb_ref[0]),
              glu(na_ref[0], nb_ref[0]), halo, ts)
    nrows = shift_ref.shape[1]
    for q in range(SUBLANES):
        shift_ref[q] = ext_ref[q:q + nrows, :]
    first = halo - (taps - 1) // 2

    def conv_body(r, carry):
        r0 = pl.multiple_of(r * rb, rb)
        for c0 in range(0, d, lc):
            lanes = slice(c0, c0 + lc)

            def tap_rows(k):
                m, q = divmod(first + k, SUBLANES)
                return shift_ref[q, pl.ds(r0 + SUBLANES * m, rb), lanes]

            conv_ref[pl.ds(r0, rb), lanes] = _dwconv_rows(tap_rows, w_ref, bdw_ref, rb, lanes, taps)
        return carry

    lax.fori_loop(0, ts // rb, conv_body, 0)

    def ln_body(r, carry):
        r0 = pl.multiple_of(r * rb, rb)
        u = conv_ref[pl.ds(r0, rb), :]
        uc = u - jnp.mean(u, axis=-1, keepdims=True)
        y = uc * lax.rsqrt(jnp.mean(uc * uc, axis=-1, keepdims=True) + EPS)
        y = y * lng_ref[...] + lnb_ref[...]
        o_ref[0, pl.ds(r0, rb), :] = _silu(y).astype(o_ref.dtype)
        return carry

    lax.fori_loop(0, ts // rb, ln_body, 0)


def _conf(proj3, col_off, b_glu, w_dw, b_dw, ln_g, ln_b):
    bsz, seq, _ = proj3.shape
    taps, d = w_dw.shape
    halo = 16
    ts = _row_tile(seq, 256)
    rb = 32
    lc = _row_tile(d, 512)
    ablk = col_off // d
    specs_a = _halo_specs(ts, halo, seq, d, lambda: ablk)
    specs_b = _halo_specs(ts, halo, seq, d, lambda: ablk + 1)
    row = lambda b_, s: (0, 0)
    return pl.pallas_call(
        functools.partial(_conf_kernel, taps, halo, ts, rb, lc),
        grid=(bsz, seq // ts),
        in_specs=specs_a + specs_b + [pl.BlockSpec((1, d), row), pl.BlockSpec((1, d), row),
                                      pl.BlockSpec((taps, d), row), pl.BlockSpec((1, d), row),
                                      pl.BlockSpec((1, d), row), pl.BlockSpec((1, d), row)],
        out_specs=pl.BlockSpec((1, ts, d), lambda b_, s: (b_, s, 0)),
        out_shape=jax.ShapeDtypeStruct((bsz, seq, d), BF16),
        scratch_shapes=[pltpu.VMEM((ts + 2 * halo, d), F32),
                        pltpu.VMEM((SUBLANES, ts + 2 * halo - SUBLANES, d), F32),
                        pltpu.VMEM((ts, d), F32)],
        compiler_params=_cp("arbitrary", "arbitrary"),
        name="conf",
    )(proj3, proj3, proj3, proj3, proj3, proj3,
      b_glu[:d].reshape(1, d), b_glu[d:].reshape(1, d), w_dw, b_dw.reshape(1, d),
      ln_g.reshape(1, d), ln_b.reshape(1, d))


def _merge_kernel(ya_ref, u_ref, wa_ref, wb_ref, bb_ref, ga_ref, gb_ref, bga_ref, bgb_ref, o_ref):
    y_a = _bdot(ya_ref[...], wa_ref[...])
    y_b = _bdot(u_ref[...], wb_ref[...]) + bb_ref[...]
    g_a = jax.nn.sigmoid(ga_ref[...] + bga_ref[...])
    g_b = jax.nn.sigmoid(gb_ref[...] + bgb_ref[...])
    o_ref[...] = (g_a * y_a + g_b * y_b).astype(o_ref.dtype)


def _merge(y_ssd, u, w_ssm_out, w_conv_out, b_conv_out, proj2, gate_off, b_gate):
    t, d_ssm = y_ssd.shape
    d_conv = u.shape[1]
    d = w_ssm_out.shape[1]
    tm = _row_tile(t, 512)
    tn = _row_tile(d, 512)
    ga = gate_off // tn
    gb = (gate_off + d) // tn
    bg = b_gate.reshape(1, 2 * d)
    return pl.pallas_call(
        _merge_kernel,
        grid=(t // tm, d // tn),
        in_specs=[pl.BlockSpec((tm, d_ssm), lambda i, j: (i, 0)),
                  pl.BlockSpec((tm, d_conv), lambda i, j: (i, 0)),
                  pl.BlockSpec((d_ssm, tn), lambda i, j: (0, j)),
                  pl.BlockSpec((d_conv, tn), lambda i, j: (0, j)),
                  pl.BlockSpec((1, tn), lambda i, j: (0, j)),
                  pl.BlockSpec((tm, tn), lambda i, j: (i, ga + j)),
                  pl.BlockSpec((tm, tn), lambda i, j: (i, gb + j)),
                  pl.BlockSpec((1, tn), lambda i, j: (0, j)),
                  pl.BlockSpec((1, tn), lambda i, j: (0, d // tn + j))],
        out_specs=pl.BlockSpec((tm, tn), lambda i, j: (i, j)),
        out_shape=jax.ShapeDtypeStruct((t, d), BF16),
        compiler_params=_cp("arbitrary", "arbitrary"),
        name="merge",
    )(y_ssd, u, w_ssm_out, w_conv_out, b_conv_out.reshape(1, d), proj2, proj2, bg, bg)


def _mixout_kernel(m_ref, w_ref, x_ref, gpost_ref, g1_ref, gpre_ref, sc2_ref, sh2_ref, x1_ref, h2_ref):
    mix = _bdot(m_ref[...], w_ref[...])
    nm = mix * lax.rsqrt(jnp.mean(mix * mix, axis=-1, keepdims=True) + EPS) * gpost_ref[...]
    x1 = x_ref[...] + g1_ref[0] * nm
    x1_ref[...] = x1
    y = x1 * lax.rsqrt(jnp.mean(x1 * x1, axis=-1, keepdims=True) + EPS) * gpre_ref[...]
    h2_ref[...] = (y * (1.0 + sc2_ref[0]) + sh2_ref[0]).astype(h2_ref.dtype)


def _mixout(mix_in, w_mix_out, x2, g_post, g1, g_pre_ffn, sc2, sh2, seq):
    t, d = x2.shape
    tm = _row_tile(seq, 512)
    per_b = seq // tm
    row = lambda i: (0, 0)
    brow = lambda i: (i // per_b, 0, 0)
    return pl.pallas_call(
        _mixout_kernel,
        grid=(t // tm,),
        in_specs=[pl.BlockSpec((tm, d), lambda i: (i, 0)),
                  pl.BlockSpec((d, d), row),
                  pl.BlockSpec((tm, d), lambda i: (i, 0)),
                  pl.BlockSpec((1, d), row),
                  pl.BlockSpec((1, 1, d), brow),
                  pl.BlockSpec((1, d), row),
                  pl.BlockSpec((1, 1, d), brow),
                  pl.BlockSpec((1, 1, d), brow)],
        out_specs=[pl.BlockSpec((tm, d), lambda i: (i, 0)),
                   pl.BlockSpec((tm, d), lambda i: (i, 0))],
        out_shape=[jax.ShapeDtypeStruct((t, d), F32), jax.ShapeDtypeStruct((t, d), BF16)],
        compiler_params=_cp("arbitrary"),
        name="mixout",
    )(mix_in, w_mix_out, x2, g_post, g1, g_pre_ffn, sc2, sh2)


def _ffn_up_kernel(h_ref, wg_ref, wu_ref, o_ref):
    h = h_ref[...]
    o_ref[...] = (_silu(_bdot(h, wg_ref[...])) * _bdot(h, wu_ref[...])).astype(o_ref.dtype)


def _ffn_up(h2, w_gate_up):
    t, d = h2.shape
    d_ff = w_gate_up.shape[1] // 2
    tm = _row_tile(t, 1024)
    tn = _row_tile(d_ff, 512)
    nj = d_ff // tn
    return pl.pallas_call(
        _ffn_up_kernel,
        grid=(t // tm, nj),
        in_specs=[pl.BlockSpec((tm, d), lambda i, j: (i, 0)),
                  pl.BlockSpec((d, tn), lambda i, j: (0, j)),
                  pl.BlockSpec((d, tn), lambda i, j: (0, nj + j))],
        out_specs=pl.BlockSpec((tm, tn), lambda i, j: (i, j)),
        out_shape=jax.ShapeDtypeStruct((t, d_ff), BF16),
        compiler_params=_cp("arbitrary", "arbitrary"),
        name="ffn_up",
    )(h2, w_gate_up, w_gate_up)


def _ffn_down_kernel(a_ref, w_ref, x1_ref, gpost_ref, g2_ref, o_ref, acc_ref):
    k = pl.program_id(1)

    @pl.when(k == 0)
    def _():
        acc_ref[...] = jnp.zeros_like(acc_ref)

    acc_ref[...] += _bdot(a_ref[...], w_ref[...])

    @pl.when(k == pl.num_programs(1) - 1)
    def _():
        f = acc_ref[...]
        nf = f * lax.rsqrt(jnp.mean(f * f, axis=-1, keepdims=True) + EPS) * gpost_ref[...]
        o_ref[...] = x1_ref[...] + g2_ref[0] * nf


def _ffn_down(act, w_down, x1, g_post, g2, seq):
    t, d_ff = act.shape
    d = w_down.shape[1]
    tm = _row_tile(seq, 512)
    tk = _row_tile(d_ff, 512)
    per_b = seq // tm
    return pl.pallas_call(
        _ffn_down_kernel,
        grid=(t // tm, d_ff // tk),
        in_specs=[pl.BlockSpec((tm, tk), lambda i, k: (i, k)),
                  pl.BlockSpec((tk, d), lambda i, k: (k, 0)),
                  pl.BlockSpec((tm, d), lambda i, k: (i, 0)),
                  pl.BlockSpec((1, d), lambda i, k: (0, 0)),
                  pl.BlockSpec((1, 1, d), lambda i, k: (i // per_b, 0, 0))],
        out_specs=pl.BlockSpec((tm, d), lambda i, k: (i, 0)),
        out_shape=jax.ShapeDtypeStruct((t, d), F32),
        scratch_shapes=[pltpu.VMEM((tm, d), F32)],
        compiler_params=_cp("arbitrary", "arbitrary"),
        name="ffn_down",
    )(act, w_down, x1, g_post, g2)


def kernel(x, c, w_ada, b_ada, g_pre_mix, g_post_mix, w_in, w_conv_ssm, b_conv_ssm, dt_bias_fwd, dt_bias_bwd, a_log_fwd, a_log_bwd, d_skip, g_ssm_norm, w_ssm_out, b_glu, w_dw, b_dw, ln_g, ln_b, w_conv_out, b_conv_out, b_gate, w_mix_out, g_pre_ffn, g_post_ffn, w_gate_up, w_down):
    bsz, seq, d = x.shape
    depth = w_ada.shape[0]
    n_heads = dt_bias_fwd.shape[1]
    d_ssm = w_ssm_out.shape[1]
    d_xbc = w_conv_ssm.shape[2]
    d_conv = w_dw.shape[2]
    pdim = d_ssm // n_heads
    assert d_xbc == d_ssm + 2 * N_GROUPS * D_STATE and seq % CHUNK == 0 and n_heads % N_GROUPS == 0
    s1, s2, s3, s4 = d_ssm, d_ssm + d_xbc, d_ssm + d_xbc + 2 * n_heads, d_ssm + d_xbc + 2 * n_heads + 2 * d_conv
    xbc_off, glu_off, gate_off = d_ssm, d_ssm + d_xbc, d_ssm + d_xbc + 2 * d_conv

    x2 = x.reshape(bsz * seq, d)
    for l in range(depth):
        w_main = jnp.concatenate([w_in[l][:, :s2], w_in[l][:, s3:]], axis=1).astype(BF16)
        w_dt = w_in[l][:, s2:s3].astype(BF16)
        row = lambda v: v.reshape(1, -1)

        mod = _ada(c, w_ada[l], b_ada[l])
        sh1, sc1, g1, sh2, sc2, g2 = [m.reshape(bsz, 1, d) for m in jnp.split(mod, 6, axis=-1)]

        proj2, dt2 = _inproj(x2, sc1, sh1, row(g_pre_mix[l]), w_main, w_dt, seq)
        proj3 = proj2.reshape(bsz, seq, -1)
        dt3 = dt2.reshape(bsz, seq, -1)

        xbc3 = _ssmconv(proj3, w_conv_ssm[l], b_conv_ssm[l], xbc_off, d_xbc)
        bias_row = row(jnp.concatenate([dt_bias_fwd[l], dt_bias_bwd[l]]))
        alog_row = row(jnp.concatenate([a_log_fwd[l], a_log_bwd[l]]))
        dskip_row = row(jnp.repeat(d_skip[l], pdim))
        y_f = _ssd(False, xbc3, dt3, bias_row, alog_row, (dskip_row,), n_heads, d_ssm)
        y_ssd = _ssd(True, xbc3, dt3, bias_row, alog_row, (y_f, proj3, row(g_ssm_norm[l])), n_heads, d_ssm)

        u = _conf(proj3, glu_off, b_glu[l], w_dw[l], b_dw[l], ln_g[l], ln_b[l])

        mix_in = _merge(y_ssd.reshape(bsz * seq, d_ssm), u.reshape(bsz * seq, d_conv),
                        w_ssm_out[l].astype(BF16), w_conv_out[l].astype(BF16), b_conv_out[l],
                        proj2, gate_off, b_gate[l])
        x1, h2 = _mixout(mix_in, w_mix_out[l].astype(BF16), x2, row(g_post_mix[l]), g1,
                         row(g_pre_ffn[l]), sc2, sh2, seq)
        act = _ffn_up(h2, w_gate_up[l].astype(BF16))
        x2 = _ffn_down(act, w_down[l].astype(BF16), x1, row(g_post_ffn[l]), g2, seq)
    return x2.reshape(bsz, seq, d)
```

```python
import functools

import jax
import jax.numpy as jnp
from jax import lax
from jax.experimental import pallas as pl
from jax.experimental.pallas import tpu as pltpu

F32 = jnp.float32
BF16 = jnp.bfloat16

EPS = 1e-6
N_GROUPS = 8
D_STATE = 128
CHUNK = 128
V7X_VMEM_LIMIT_BYTES = 56 * 1024 * 1024
SUBLANES = 8
LANES = 128
LOG2E = 1.4426950408889634


def _cp(*sem):
    return pltpu.CompilerParams(dimension_semantics=sem, vmem_limit_bytes=V7X_VMEM_LIMIT_BYTES)


def _silu(v):
    return v * jax.nn.sigmoid(v)


def _softplus(v):
    return jnp.maximum(v, 0.0) + jnp.log1p(jnp.exp(-jnp.abs(v)))


def _bdot(a, b):
    return jnp.dot(a, b, preferred_element_type=F32)


def _row_tile(n, want):
    t = min(n, want)
    assert n % t == 0, (n, t)
    return t


def _lane_tile(n, want):
    return max(t for t in range(LANES, min(n, want) + 1, LANES) if n % t == 0)


def _ada_kernel(c_ref, w_ref, b_ref, o_ref):
    ca = _silu(c_ref[...]).astype(BF16)
    o_ref[...] = _bdot(ca, w_ref[...].astype(BF16)) + b_ref[...]


def _ada(c, w, b):
    bsz, d = c.shape
    n = w.shape[1]
    rows = 16
    tn = _row_tile(n, 1024)
    cpad = jnp.zeros((rows, d), F32).at[:bsz].set(c)
    out = pl.pallas_call(
        _ada_kernel,
        grid=(n // tn,),
        in_specs=[pl.BlockSpec((rows, d), lambda j: (0, 0)),
                  pl.BlockSpec((d, tn), lambda j: (0, j)),
                  pl.BlockSpec((1, tn), lambda j: (0, j))],
        out_specs=pl.BlockSpec((rows, tn), lambda j: (0, j)),
        out_shape=jax.ShapeDtypeStruct((rows, n), F32),
        compiler_params=_cp("arbitrary"),
        name="ada",
    )(cpad, w, b.reshape(1, n))
    return out[:bsz]


def _inproj_kernel(x_ref, sc_ref, sh_ref, g_ref, w_ref, wdt_ref, o_ref, odt_ref, h_ref):
    @pl.when(pl.program_id(1) == 0)
    def _():
        x = x_ref[...]
        y = x * lax.rsqrt(jnp.mean(x * x, axis=-1, keepdims=True) + EPS) * g_ref[...]
        h = (y * (1.0 + sc_ref[0]) + sh_ref[0]).astype(BF16)
        h_ref[...] = h
        odt_ref[...] = _bdot(h, wdt_ref[...])

    o_ref[...] = _bdot(h_ref[...], w_ref[...])


def _inproj(x2, sc, sh, g, w_main, w_dt, seq):
    t, d = x2.shape
    n = w_main.shape[1]
    ndt = w_dt.shape[1]
    tm = _row_tile(seq, 1024)
    tn = _row_tile(n, 1024)
    per_b = seq // tm
    return pl.pallas_call(
        _inproj_kernel,
        grid=(t // tm, n // tn),
        in_specs=[pl.BlockSpec((tm, d), lambda i, j: (i, 0)),
                  pl.BlockSpec((1, 1, d), lambda i, j: (i // per_b, 0, 0)),
                  pl.BlockSpec((1, 1, d), lambda i, j: (i // per_b, 0, 0)),
                  pl.BlockSpec((1, d), lambda i, j: (0, 0)),
                  pl.BlockSpec((d, tn), lambda i, j: (0, j)),
                  pl.BlockSpec((d, ndt), lambda i, j: (0, 0))],
        out_specs=[pl.BlockSpec((tm, tn), lambda i, j: (i, j)),
                   pl.BlockSpec((tm, ndt), lambda i, j: (i, 0))],
        out_shape=[jax.ShapeDtypeStruct((t, n), F32),
                   jax.ShapeDtypeStruct((t, ndt), F32)],
        scratch_shapes=[pltpu.VMEM((tm, d), BF16)],
        compiler_params=_cp("arbitrary", "arbitrary"),
        name="inproj",
    )(x2, sc, sh, g, w_main, w_dt)


def _fill_ext(ext_ref, prev, main, nxt, halo, ts):
    s = pl.program_id(1)
    last = pl.num_programs(1) - 1
    ext_ref[0:halo, :] = jnp.where(s > 0, prev, 0.0)
    ext_ref[halo:halo + ts, :] = main
    ext_ref[halo + ts:halo + ts + halo, :] = jnp.where(s < last, nxt, 0.0)


def _dwconv_rows(tap_rows, w_ref, b_ref, rb, lanes, taps):
    acc = jnp.broadcast_to(b_ref[:, lanes], (rb, lanes.stop - lanes.start))
    for k in range(taps):
        acc = acc + tap_rows(k) * w_ref[k:k + 1, lanes]
    return acc


def _halo_specs(ts, halo, seq, width, col_block):
    per = ts // halo
    nh = seq // halo

    def prev_map(b, s, *c):
        return (b, jnp.maximum(s * per - 1, 0), col_block(*c))

    def main_map(b, s, *c):
        return (b, s, col_block(*c))

    def next_map(b, s, *c):
        return (b, jnp.minimum((s + 1) * per, nh - 1), col_block(*c))

    return [pl.BlockSpec((1, halo, width), prev_map),
            pl.BlockSpec((1, ts, width), main_map),
            pl.BlockSpec((1, halo, width), next_map)]


def _ssmconv_kernel(taps, halo, ts, rb, prev_ref, main_ref, next_ref, w_ref, b_ref, o_ref, ext_ref, shift_ref):
    _fill_ext(ext_ref, prev_ref[0], main_ref[0], next_ref[0], halo, ts)
    lanes = slice(0, o_ref.shape[-1])
    first = halo - (taps - 1) // 2
    nrows = shift_ref.shape[1]
    shifts = sorted({(first + k) % SUBLANES for k in range(taps)} - {0})
    for n, q in enumerate(shifts):
        shift_ref[n] = ext_ref[q:q + nrows, :]

    def tap_rows(r0, k):
        m, q = divmod(first + k, SUBLANES)
        if q == 0:
            return ext_ref[r0 + SUBLANES * m:r0 + SUBLANES * m + rb, lanes]
        return shift_ref[shifts.index(q), r0 + SUBLANES * m:r0 + SUBLANES * m + rb, lanes]

    for r0 in range(0, ts, rb):
        acc = _dwconv_rows(functools.partial(tap_rows, r0), w_ref, b_ref, rb, lanes, taps)
        o_ref[0, r0:r0 + rb, :] = _silu(acc)


def _ssmconv(proj3, w, b, col_off, width):
    bsz, seq, _ = proj3.shape
    taps = w.shape[0]
    halo = 8
    ts = _row_tile(seq, 512)
    tc = _row_tile(width, 512)
    rb = 32
    off = col_off // tc
    specs = _halo_specs(ts, halo, seq, tc, lambda c: off + c)
    return pl.pallas_call(
        functools.partial(_ssmconv_kernel, taps, halo, ts, rb),
        grid=(bsz, seq // ts, width // tc),
        in_specs=specs + [pl.BlockSpec((taps, tc), lambda b_, s, c: (0, c)),
                          pl.BlockSpec((1, tc), lambda b_, s, c: (0, c))],
        out_specs=pl.BlockSpec((1, ts, tc), lambda b_, s, c: (b_, s, c)),
        out_shape=jax.ShapeDtypeStruct((bsz, seq, width), F32),
        scratch_shapes=[pltpu.VMEM((ts + 2 * halo, tc), F32),
                        pltpu.VMEM((min(taps, SUBLANES) - 1, ts + 2 * halo - SUBLANES, tc), F32)],
        compiler_params=_cp("arbitrary", "arbitrary", "arbitrary"),
        name="ssmconv",
    )(proj3, proj3, proj3, w, b.reshape(1, width))


def _ssd_kernel(reverse, n_heads, hpg, pdim, *refs):
    if reverse:
        (xs_ref, b_ref, c_ref, dt_ref, bias_ref, alog_ref, yf_ref, z_ref, gn_ref,
         o_ref, h_ref, dt_t, cs_t, wend_t, cdec_t) = refs
    else:
        (xs_ref, b_ref, c_ref, dt_ref, bias_ref, alog_ref, dskip_ref,
         o_ref, h_ref, dt_t, cs_t, wend_t, cdec_t) = refs
    ln = CHUNK
    w = hpg * pdim
    hd2 = 2 * n_heads
    nt = (((1,), (1,)), ((), ()))

    @pl.when(pl.program_id(1) == 0)
    def _():
        h_ref[...] = jnp.zeros_like(h_ref)

    ri = lax.broadcasted_iota(jnp.int32, (ln, ln), 0)
    ci = lax.broadcasted_iota(jnp.int32, (ln, ln), 1)
    mask = (ci >= ri) if reverse else (ci <= ri)
    cum = mask.astype(BF16)
    edge = 0 if reverse else ln - 1

    dt_all = _softplus(dt_ref[0] + bias_ref[...])
    a_all = dt_all * (-jnp.exp(alog_ref[...]) * LOG2E)
    a_hi = a_all.astype(BF16)
    r1 = a_all - a_hi.astype(F32)
    a_mid = r1.astype(BF16)
    a_lo = (r1 - a_mid.astype(F32)).astype(BF16)
    cs_all = _bdot(cum, a_hi) + _bdot(cum, a_mid) + _bdot(cum, a_lo)
    cs_edge = cs_all[edge:edge + 1, :]
    dt_t[...] = dt_all.T
    cs_t[...] = cs_all.T
    wend_t[...] = (dt_all * jnp.exp2(cs_edge - cs_all)).T
    cdec_t[...] = jnp.broadcast_to(jnp.exp2(cs_edge), (ln, hd2)).T

    dirbase = n_heads if reverse else 0
    for g in range(N_GROUPS):
        lanes = slice(g * w, (g + 1) * w)
        xs = xs_ref[0, :, lanes]
        x_t = xs.T
        bgb = b_ref[0, :, g * D_STATE:(g + 1) * D_STATE].astype(BF16)
        cgb = c_ref[0, :, g * D_STATE:(g + 1) * D_STATE].astype(BF16)
        cb = lax.dot_general(cgb, bgb, nt, preferred_element_type=F32)
        h_in = h_ref[g]
        yoff = lax.dot_general(cgb, h_in.astype(BF16), nt, preferred_element_type=F32)
        y_parts, xw_parts, cdec_parts = [], [], []
        for jp in range(hpg // 2):
            ms, xdts, cols = [], [], []
            for j in (2 * jp, 2 * jp + 1):
                c = dirbase + g * hpg + j
                rows = slice(j * pdim, (j + 1) * pdim)
                xdts.append((x_t[rows] * dt_t[c:c + 1, :]).astype(BF16))
                xw_parts.append((x_t[rows] * wend_t[c:c + 1, :]).astype(BF16))
                col = jnp.broadcast_to(cs_all[:, c:c + 1], (ln, ln))
                seg = col - cs_t[c:c + 1, :]
                ms.append((cb * jnp.where(mask, jnp.exp2(seg), 0.0)).astype(BF16))
                cols.append(col)
                cdec_parts.append(jnp.broadcast_to(cdec_t[c:c + 1, :], (pdim, D_STATE)))
            zero = jnp.zeros((pdim, ln), BF16)
            xdt_bd = jnp.concatenate([jnp.concatenate([xdts[0], zero], axis=1),
                                      jnp.concatenate([zero, xdts[1]], axis=1)], axis=0)
            yd = lax.dot_general(jnp.concatenate(ms, axis=1), xdt_bd, nt,
                                 preferred_element_type=F32)
            din = jnp.exp2(jnp.where(ci < pdim, cols[0], cols[1]))
            y_parts.append(yd + yoff[:, 2 * jp * pdim:(2 * jp + 2) * pdim] * din)
        y = jnp.concatenate(y_parts, axis=1)
        st = _bdot(jnp.concatenate(xw_parts, axis=0), bgb)
        h_ref[g] = h_in * jnp.concatenate(cdec_parts, axis=0) + st
        if reverse:
            y = (y + yf_ref[0, :, lanes]) * _silu(z_ref[0, :, lanes])
            y = y * lax.rsqrt(jnp.mean(y * y, axis=-1, keepdims=True) + EPS) * gn_ref[:, lanes]
            o_ref[0, :, lanes] = y.astype(o_ref.dtype)
        else:
            o_ref[0, :, lanes] = y + dskip_ref[:, lanes] * xs


def _ssd(reverse, xbc3, dt3, bias_row, alog_row, extra, n_heads, d_ssm):
    bsz, seq, _ = xbc3.shape
    hd2 = 2 * n_heads
    hpg = n_heads // N_GROUPS
    pdim = d_ssm // n_heads
    w = hpg * pdim
    nc = seq // CHUNK
    ln = CHUNK
    gn = N_GROUPS * D_STATE
    assert CHUNK == D_STATE and d_ssm % gn == 0 and 2 * pdim == CHUNK and hpg % 2 == 0

    def chunk(b, z):
        return (b, nc - 1 - z if reverse else z)

    in_specs = [pl.BlockSpec((1, ln, d_ssm), lambda b, z: (*chunk(b, z), 0)),
                pl.BlockSpec((1, ln, gn), lambda b, z: (*chunk(b, z), d_ssm // gn)),
                pl.BlockSpec((1, ln, gn), lambda b, z: (*chunk(b, z), d_ssm // gn + 1)),
                pl.BlockSpec((1, ln, hd2), lambda b, z: (*chunk(b, z), 0)),
                pl.BlockSpec((1, hd2), lambda b, z: (0, 0)),
                pl.BlockSpec((1, hd2), lambda b, z: (0, 0))]
    args = [xbc3, xbc3, xbc3, dt3, bias_row, alog_row]
    if reverse:
        y_f, proj3, gnorm = extra
        in_specs += [pl.BlockSpec((1, ln, d_ssm), lambda b, z: (*chunk(b, z), 0)),
                     pl.BlockSpec((1, ln, d_ssm), lambda b, z: (*chunk(b, z), 0)),
                     pl.BlockSpec((1, d_ssm), lambda b, z: (0, 0))]
        args += [y_f, proj3, gnorm]
        out_dtype = BF16
    else:
        (dskip,) = extra
        in_specs += [pl.BlockSpec((1, d_ssm), lambda b, z: (0, 0))]
        args += [dskip]
        out_dtype = F32
    table = pltpu.VMEM((hd2, ln), F32)
    return pl.pallas_call(
        functools.partial(_ssd_kernel, reverse, n_heads, hpg, pdim),
        grid=(bsz, nc),
        in_specs=in_specs,
        out_specs=pl.BlockSpec((1, ln, d_ssm), lambda b, z: (*chunk(b, z), 0)),
        out_shape=jax.ShapeDtypeStruct((bsz, seq, d_ssm), out_dtype),
        scratch_shapes=[pltpu.VMEM((N_GROUPS, w, D_STATE), F32), table, table, table, table],
        compiler_params=_cp("arbitrary", "arbitrary"),
        name="ssd_bwd" if reverse else "ssd_fwd",
    )(*args)


def _conf_kernel(taps, halo, ts, rb, lc, pa_ref, ma_ref, na_ref, pb_ref, mb_ref, nb_ref,
                 bga_ref, bgb_ref, w_ref, bdw_ref, lng_ref, lnb_ref, o_ref, ext_ref, shift_ref, conv_ref):
    d = o_ref.shape[-1]

    def glu(a, b):
        return (a + bga_ref[...]) * jax.nn.sigmoid(b + bgb_ref[...])

    _fill_ext(ext_ref, glu(pa_ref[0], pb_ref[0]), glu(ma_ref[0], mb_ref[0]),
              glu(na_ref[0], nb_ref[0]), halo, ts)
    nrows = shift_ref.shape[1]
    for q in range(SUBLANES):
        shift_ref[q] = ext_ref[q:q + nrows, :]
    first = halo - (taps - 1) // 2

    for c0 in range(0, d, lc):
        lanes = slice(c0, c0 + lc)
        nblk = ts // SUBLANES
        acc = [jnp.broadcast_to(bdw_ref[:, lanes], (SUBLANES, lc))] * nblk
        for q in range(SUBLANES):
            ms = [(first + k) // SUBLANES for k in range(taps) if (first + k) % SUBLANES == q]
            w_b = {m: jnp.broadcast_to(w_ref[SUBLANES * m + q - first:SUBLANES * m + q - first + 1, lanes],
                                       (SUBLANES, lc)) for m in ms}
            for i in range(nblk + max(ms)):
                blk = shift_ref[q, SUBLANES * i:SUBLANES * (i + 1), lanes]
                for m in ms:
                    if 0 <= i - m < nblk:
                        acc[i - m] = acc[i - m] + blk * w_b[m]
        for i in range(nblk):
            conv_ref[SUBLANES * i:SUBLANES * (i + 1), lanes] = acc[i]

    u = conv_ref[...]
    uc = u - jnp.mean(u, axis=-1, keepdims=True)
    y = uc * lax.rsqrt(jnp.mean(uc * uc, axis=-1, keepdims=True) + EPS)
    y = y * lng_ref[...] + lnb_ref[...]
    o_ref[0] = _silu(y).astype(o_ref.dtype)


def _conf(proj3, col_off, b_glu, w_dw, b_dw, ln_g, ln_b):
    bsz, seq, _ = proj3.shape
    taps, d = w_dw.shape
    halo = 16
    ts = _row_tile(seq, 256)
    rb = 64
    lc = 128
    ablk = col_off // d
    specs_a = _halo_specs(ts, halo, seq, d, lambda: ablk)
    specs_b = _halo_specs(ts, halo, seq, d, lambda: ablk + 1)
    row = lambda b_, s: (0, 0)
    return pl.pallas_call(
        functools.partial(_conf_kernel, taps, halo, ts, rb, lc),
        grid=(bsz, seq // ts),
        in_specs=specs_a + specs_b + [pl.BlockSpec((1, d), row), pl.BlockSpec((1, d), row),
                                      pl.BlockSpec((taps, d), row), pl.BlockSpec((1, d), row),
                                      pl.BlockSpec((1, d), row), pl.BlockSpec((1, d), row)],
        out_specs=pl.BlockSpec((1, ts, d), lambda b_, s: (b_, s, 0)),
        out_shape=jax.ShapeDtypeStruct((bsz, seq, d), BF16),
        scratch_shapes=[pltpu.VMEM((ts + 2 * halo, d), F32),
                        pltpu.VMEM((SUBLANES, ts + 2 * halo - SUBLANES, d), F32),
                        pltpu.VMEM((ts, d), F32)],
        compiler_params=_cp("arbitrary", "arbitrary"),
        name="conf",
    )(proj3, proj3, proj3, proj3, proj3, proj3,
      b_glu[:d].reshape(1, d), b_glu[d:].reshape(1, d), w_dw, b_dw.reshape(1, d),
      ln_g.reshape(1, d), ln_b.reshape(1, d))


def _merge_kernel(ya_ref, u_ref, wa_ref, wb_ref, bb_ref, ga_ref, gb_ref, bga_ref, bgb_ref, o_ref):
    y_a = _bdot(ya_ref[...], wa_ref[...])
    y_b = _bdot(u_ref[...], wb_ref[...]) + bb_ref[...]
    g_a = jax.nn.sigmoid(ga_ref[...] + bga_ref[...])
    g_b = jax.nn.sigmoid(gb_ref[...] + bgb_ref[...])
    o_ref[...] = (g_a * y_a + g_b * y_b).astype(o_ref.dtype)


def _merge(y_ssd, u, w_ssm_out, w_conv_out, b_conv_out, proj2, gate_off, b_gate):
    t, d_ssm = y_ssd.shape
    d_conv = u.shape[1]
    d = w_ssm_out.shape[1]
    tm = _row_tile(t, 1024)
    tn = _row_tile(d, 512)
    ga = gate_off // tn
    gb = (gate_off + d) // tn
    bg = b_gate.reshape(1, 2 * d)
    return pl.pallas_call(
        _merge_kernel,
        grid=(t // tm, d // tn),
        in_specs=[pl.BlockSpec((tm, d_ssm), lambda i, j: (i, 0)),
                  pl.BlockSpec((tm, d_conv), lambda i, j: (i, 0)),
                  pl.BlockSpec((d_ssm, tn), lambda i, j: (0, j)),
                  pl.BlockSpec((d_conv, tn), lambda i, j: (0, j)),
                  pl.BlockSpec((1, tn), lambda i, j: (0, j)),
                  pl.BlockSpec((tm, tn), lambda i, j: (i, ga + j)),
                  pl.BlockSpec((tm, tn), lambda i, j: (i, gb + j)),
                  pl.BlockSpec((1, tn), lambda i, j: (0, j)),
                  pl.BlockSpec((1, tn), lambda i, j: (0, d // tn + j))],
        out_specs=pl.BlockSpec((tm, tn), lambda i, j: (i, j)),
        out_shape=jax.ShapeDtypeStruct((t, d), BF16),
        compiler_params=_cp("arbitrary", "arbitrary"),
        name="merge",
    )(y_ssd, u, w_ssm_out, w_conv_out, b_conv_out.reshape(1, d), proj2, proj2, bg, bg)


def _mixout_kernel(m_ref, w_ref, x_ref, gpost_ref, g1_ref, gpre_ref, sc2_ref, sh2_ref, x1_ref, h2_ref):
    mix = _bdot(m_ref[...], w_ref[...])
    nm = mix * lax.rsqrt(jnp.mean(mix * mix, axis=-1, keepdims=True) + EPS) * gpost_ref[...]
    x1 = x_ref[...] + g1_ref[0] * nm
    x1_ref[...] = x1
    y = x1 * lax.rsqrt(jnp.mean(x1 * x1, axis=-1, keepdims=True) + EPS) * gpre_ref[...]
    h2_ref[...] = (y * (1.0 + sc2_ref[0]) + sh2_ref[0]).astype(h2_ref.dtype)


def _mixout(mix_in, w_mix_out, x2, g_post, g1, g_pre_ffn, sc2, sh2, seq):
    t, d = x2.shape
    tm = _row_tile(seq, 512)
    per_b = seq // tm
    row = lambda i: (0, 0)
    brow = lambda i: (i // per_b, 0, 0)
    return pl.pallas_call(
        _mixout_kernel,
        grid=(t // tm,),
        in_specs=[pl.BlockSpec((tm, d), lambda i: (i, 0)),
                  pl.BlockSpec((d, d), row),
                  pl.BlockSpec((tm, d), lambda i: (i, 0)),
                  pl.BlockSpec((1, d), row),
                  pl.BlockSpec((1, 1, d), brow),
                  pl.BlockSpec((1, d), row),
                  pl.BlockSpec((1, 1, d), brow),
                  pl.BlockSpec((1, 1, d), brow)],
        out_specs=[pl.BlockSpec((tm, d), lambda i: (i, 0)),
                   pl.BlockSpec((tm, d), lambda i: (i, 0))],
        out_shape=[jax.ShapeDtypeStruct((t, d), F32), jax.ShapeDtypeStruct((t, d), BF16)],
        compiler_params=_cp("arbitrary"),
        name="mixout",
    )(mix_in, w_mix_out, x2, g_post, g1, g_pre_ffn, sc2, sh2)


def _ffn_up_kernel(h_ref, wg_ref, wu_ref, o_ref):
    h = h_ref[...]
    o_ref[...] = (_silu(_bdot(h, wg_ref[...])) * _bdot(h, wu_ref[...])).astype(o_ref.dtype)


def _ffn_up(h2, w_gate_up):
    t, d = h2.shape
    d_ff = w_gate_up.shape[1] // 2
    tm = _row_tile(t, 1024)
    tn = _row_tile(d_ff, 512)
    nj = d_ff // tn
    return pl.pallas_call(
        _ffn_up_kernel,
        grid=(t // tm, nj),
        in_specs=[pl.BlockSpec((tm, d), lambda i, j: (i, 0)),
                  pl.BlockSpec((d, tn), lambda i, j: (0, j)),
                  pl.BlockSpec((d, tn), lambda i, j: (0, nj + j))],
        out_specs=pl.BlockSpec((tm, tn), lambda i, j: (i, j)),
        out_shape=jax.ShapeDtypeStruct((t, d_ff), BF16),
        compiler_params=_cp("arbitrary", "arbitrary"),
        name="ffn_up",
    )(h2, w_gate_up, w_gate_up)


def _ffn_down_kernel(a_ref, w_ref, x1_ref, gpost_ref, g2_ref, o_ref):
    k = pl.program_id(1)
    part = _bdot(a_ref[...], w_ref[...])

    @pl.when(k == 0)
    def _():
        o_ref[...] = part

    @pl.when(k > 0)
    def _():
        o_ref[...] += part

    @pl.when(k == pl.num_programs(1) - 1)
    def _():
        f = o_ref[...]
        nf = f * lax.rsqrt(jnp.mean(f * f, axis=-1, keepdims=True) + EPS) * gpost_ref[...]
        o_ref[...] = x1_ref[...] + g2_ref[0] * nf


def _ffn_down(act, w_down, x1, g_post, g2, seq):
    t, d_ff = act.shape
    d = w_down.shape[1]
    tm = _row_tile(seq, 1024)
    tk = _lane_tile(d_ff, 512)
    per_b = seq // tm
    return pl.pallas_call(
        _ffn_down_kernel,
        grid=(t // tm, d_ff // tk),
        in_specs=[pl.BlockSpec((tm, tk), lambda i, k: (i, k)),
                  pl.BlockSpec((tk, d), lambda i, k: (k, 0)),
                  pl.BlockSpec((tm, d), lambda i, k: (i, 0), pipeline_mode=pl.Buffered(1)),
                  pl.BlockSpec((1, d), lambda i, k: (0, 0)),
                  pl.BlockSpec((1, 1, d), lambda i, k: (i // per_b, 0, 0))],
        out_specs=pl.BlockSpec((tm, d), lambda i, k: (i, 0)),
        out_shape=jax.ShapeDtypeStruct((t, d), F32),
        compiler_params=_cp("arbitrary", "arbitrary"),
        name="ffn_down",
    )(act, w_down, x1, g_post, g2)


def kernel(x, c, w_ada, b_ada, g_pre_mix, g_post_mix, w_in, w_conv_ssm, b_conv_ssm, dt_bias_fwd, dt_bias_bwd, a_log_fwd, a_log_bwd, d_skip, g_ssm_norm, w_ssm_out, b_glu, w_dw, b_dw, ln_g, ln_b, w_conv_out, b_conv_out, b_gate, w_mix_out, g_pre_ffn, g_post_ffn, w_gate_up, w_down):
    bsz, seq, d = x.shape
    depth = w_ada.shape[0]
    n_heads = dt_bias_fwd.shape[1]
    d_ssm = w_ssm_out.shape[1]
    d_xbc = w_conv_ssm.shape[2]
    d_conv = w_dw.shape[2]
    pdim = d_ssm // n_heads
    assert d_xbc == d_ssm + 2 * N_GROUPS * D_STATE and seq % CHUNK == 0 and n_heads % N_GROUPS == 0
    s1, s2, s3, s4 = d_ssm, d_ssm + d_xbc, d_ssm + d_xbc + 2 * n_heads, d_ssm + d_xbc + 2 * n_heads + 2 * d_conv
    xbc_off, glu_off, gate_off = d_ssm, d_ssm + d_xbc, d_ssm + d_xbc + 2 * d_conv

    x2 = x.reshape(bsz * seq, d)
    for l in range(depth):
        w_main = jnp.concatenate([w_in[l][:, :s2], w_in[l][:, s3:]], axis=1).astype(BF16)
        w_dt = w_in[l][:, s2:s3].astype(BF16)
        row = lambda v: v.reshape(1, -1)

        mod = _ada(c, w_ada[l], b_ada[l])
        sh1, sc1, g1, sh2, sc2, g2 = [m.reshape(bsz, 1, d) for m in jnp.split(mod, 6, axis=-1)]

        proj2, dt2 = _inproj(x2, sc1, sh1, row(g_pre_mix[l]), w_main, w_dt, seq)
        proj3 = proj2.reshape(bsz, seq, -1)
        dt3 = dt2.reshape(bsz, seq, -1)

        xbc3 = _ssmconv(proj3, w_conv_ssm[l], b_conv_ssm[l], xbc_off, d_xbc)
        bias_row = row(jnp.concatenate([dt_bias_fwd[l], dt_bias_bwd[l]]))
        alog_row = row(jnp.concatenate([a_log_fwd[l], a_log_bwd[l]]))
        dskip_row = row(jnp.repeat(d_skip[l], pdim))
        y_f = _ssd(False, xbc3, dt3, bias_row, alog_row, (dskip_row,), n_heads, d_ssm)
        y_ssd = _ssd(True, xbc3, dt3, bias_row, alog_row, (y_f, proj3, row(g_ssm_norm[l])), n_heads, d_ssm)

        u = _conf(proj3, glu_off, b_glu[l], w_dw[l], b_dw[l], ln_g[l], ln_b[l])

        mix_in = _merge(y_ssd.reshape(bsz * seq, d_ssm), u.reshape(bsz * seq, d_conv),
                        w_ssm_out[l].astype(BF16), w_conv_out[l].astype(BF16), b_conv_out[l],
                        proj2, gate_off, b_gate[l])
        x1, h2 = _mixout(mix_in, w_mix_out[l].astype(BF16), x2, row(g_post_mix[l]), g1,
                         row(g_pre_ffn[l]), sc2, sh2, seq)
        act = _ffn_up(h2, w_gate_up[l].astype(BF16))
        x2 = _ffn_down(act, w_down[l].astype(BF16), x1, row(g_post_ffn[l]), g2, seq)
    return x2.reshape(bsz, seq, d)
```

```python
import functools

import jax
import jax.numpy as jnp
from jax import lax
from jax.experimental import pallas as pl
from jax.experimental.pallas import tpu as pltpu

F32 = jnp.float32
BF16 = jnp.bfloat16

EPS = 1e-6
N_GROUPS = 8
D_STATE = 128
CHUNK = 128
V7X_VMEM_LIMIT_BYTES = 56 * 1024 * 1024
SUBLANES = 8
LANES = 128
LOG2E = 1.4426950408889634


def _cp(*sem):
    return pltpu.CompilerParams(dimension_semantics=sem, vmem_limit_bytes=V7X_VMEM_LIMIT_BYTES)


def _silu(v):
    return v * jax.nn.sigmoid(v)


def _softplus(v):
    return jnp.maximum(v, 0.0) + jnp.log1p(jnp.exp(-jnp.abs(v)))


def _bdot(a, b):
    return jnp.dot(a, b, preferred_element_type=F32)


def _row_tile(n, want):
    t = min(n, want)
    assert n % t == 0, (n, t)
    return t


def _lane_tile(n, want):
    return max(t for t in range(LANES, min(n, want) + 1, LANES) if n % t == 0)


def _ada_kernel(c_ref, w_ref, b_ref, o_ref):
    ca = _silu(c_ref[...]).astype(BF16)
    o_ref[...] = _bdot(ca, w_ref[...].astype(BF16)) + b_ref[...]


def _ada(c, w, b):
    bsz, d = c.shape
    n = w.shape[1]
    rows = 16
    tn = _row_tile(n, 1024)
    cpad = jnp.zeros((rows, d), F32).at[:bsz].set(c)
    out = pl.pallas_call(
        _ada_kernel,
        grid=(n // tn,),
        in_specs=[pl.BlockSpec((rows, d), lambda j: (0, 0)),
                  pl.BlockSpec((d, tn), lambda j: (0, j)),
                  pl.BlockSpec((1, tn), lambda j: (0, j))],
        out_specs=pl.BlockSpec((rows, tn), lambda j: (0, j)),
        out_shape=jax.ShapeDtypeStruct((rows, n), F32),
        compiler_params=_cp("arbitrary"),
        name="ada",
    )(cpad, w, b.reshape(1, n))
    return out[:bsz]


def _inproj_kernel(x_ref, sc_ref, sh_ref, g_ref, w_ref, wdt_ref, o_ref, odt_ref, h_ref):
    @pl.when(pl.program_id(1) == 0)
    def _():
        x = x_ref[...]
        y = x * lax.rsqrt(jnp.mean(x * x, axis=-1, keepdims=True) + EPS) * g_ref[...]
        h = (y * (1.0 + sc_ref[0]) + sh_ref[0]).astype(BF16)
        h_ref[...] = h
        odt_ref[...] = _bdot(h, wdt_ref[...])

    o_ref[...] = _bdot(h_ref[...], w_ref[...])


def _inproj(x2, sc, sh, g, w_main, w_dt, seq):
    t, d = x2.shape
    n = w_main.shape[1]
    ndt = w_dt.shape[1]
    tm = _row_tile(seq, 1024)
    tn = _row_tile(n, 1024)
    per_b = seq // tm
    return pl.pallas_call(
        _inproj_kernel,
        grid=(t // tm, n // tn),
        in_specs=[pl.BlockSpec((tm, d), lambda i, j: (i, 0)),
                  pl.BlockSpec((1, 1, d), lambda i, j: (i // per_b, 0, 0)),
                  pl.BlockSpec((1, 1, d), lambda i, j: (i // per_b, 0, 0)),
                  pl.BlockSpec((1, d), lambda i, j: (0, 0)),
                  pl.BlockSpec((d, tn), lambda i, j: (0, j)),
                  pl.BlockSpec((d, ndt), lambda i, j: (0, 0))],
        out_specs=[pl.BlockSpec((tm, tn), lambda i, j: (i, j)),
                   pl.BlockSpec((tm, ndt), lambda i, j: (i, 0))],
        out_shape=[jax.ShapeDtypeStruct((t, n), F32),
                   jax.ShapeDtypeStruct((t, ndt), F32)],
        scratch_shapes=[pltpu.VMEM((tm, d), BF16)],
        compiler_params=_cp("arbitrary", "arbitrary"),
        name="inproj",
    )(x2, sc, sh, g, w_main, w_dt)


def _fill_ext(ext_ref, prev, main, nxt, halo, ts):
    s = pl.program_id(1)
    last = pl.num_programs(1) - 1
    ext_ref[0:halo, :] = jnp.where(s > 0, prev, 0.0)
    ext_ref[halo:halo + ts, :] = main
    ext_ref[halo + ts:halo + ts + halo, :] = jnp.where(s < last, nxt, 0.0)


def _dwconv_rows(tap_rows, w_ref, b_ref, rb, lanes, taps):
    acc = jnp.broadcast_to(b_ref[:, lanes], (rb, lanes.stop - lanes.start))
    for k in range(taps):
        acc = acc + tap_rows(k) * w_ref[k:k + 1, lanes]
    return acc


def _halo_specs(ts, halo, seq, width, col_block):
    per = ts // halo
    nh = seq // halo

    def prev_map(b, s, *c):
        return (b, jnp.maximum(s * per - 1, 0), col_block(*c))

    def main_map(b, s, *c):
        return (b, s, col_block(*c))

    def next_map(b, s, *c):
        return (b, jnp.minimum((s + 1) * per, nh - 1), col_block(*c))

    return [pl.BlockSpec((1, halo, width), prev_map),
            pl.BlockSpec((1, ts, width), main_map),
            pl.BlockSpec((1, halo, width), next_map)]


def _xbc_kernel(taps, halo, tm, ncol, per_b, strip, rb, kc,
                xp_ref, xm_ref, xn_ref, sc_ref, sh_ref, g_ref, w_ref, cw_ref, cb_ref,
                o_ref, h_ref, raw_a, raw_b, shift_ref):
    s = pl.program_id(0)
    ntiles = pl.num_programs(0) - 1
    cur = jnp.minimum(s, ntiles - 1)
    tn = o_ref.shape[-1]
    first = halo - (taps - 1) // 2
    shifts = sorted({(first + k) % SUBLANES for k in range(taps)} - {0})
    nrows = shift_ref.shape[1]

    @pl.when(s == 0)
    def _():
        raw_b[...] = jnp.zeros_like(raw_b)

    @pl.when((s < ntiles) & (cur % ncol == 0))
    def _():
        def modulated(x):
            y = x * lax.rsqrt(jnp.mean(x * x, axis=-1, keepdims=True) + EPS) * g_ref[...]
            return y * (1.0 + sc_ref[0]) + sh_ref[0]

        h_ref[...] = jnp.concatenate(
            [modulated(xp_ref[...]), modulated(xm_ref[...]), modulated(xn_ref[...])], axis=0).astype(BF16)

    def step(raw_w, raw_r):
        pos = (jnp.maximum(s - 1, 0) // ncol) % per_b
        raw_r[0:halo, :] = jnp.where(pos > 0, raw_r[0:halo, :], 0.0)
        raw_r[halo + tm:, :] = jnp.where(pos < per_b - 1, raw_r[halo + tm:, :], 0.0)
        pieces = []
        for c0 in range(0, tn, strip):
            lanes = slice(c0, c0 + strip)

            def copy_shifts(lanes=lanes):
                for n, q in enumerate(shifts):
                    shift_ref[n] = raw_r[q:q + nrows, lanes]

            def tap_rows(r0, k, lanes=lanes):
                m, q = divmod(first + k, SUBLANES)
                rows = slice(r0 + SUBLANES * m, r0 + SUBLANES * m + rb)
                return raw_r[rows, lanes] if q == 0 else shift_ref[shifts.index(q), rows, :]

            def conv_rows(r0, lanes=lanes, tap_rows=tap_rows):
                acc = _dwconv_rows(functools.partial(tap_rows, r0), cw_ref, cb_ref, rb, lanes, taps)
                o_ref[r0:r0 + rb, lanes] = _silu(acc)

            pieces.append(copy_shifts)
            pieces += [functools.partial(conv_rows, r0) for r0 in range(0, tm, rb)]
        nk = h_ref.shape[1] // kc
        per_chunk = -(-len(pieces) // nk)
        for c in range(nk):
            part = _bdot(h_ref[:, c * kc:(c + 1) * kc], w_ref[c * kc:(c + 1) * kc, :])
            if c == 0:
                raw_w[...] = part
            else:
                raw_w[...] += part
            for piece in pieces[c * per_chunk:(c + 1) * per_chunk]:
                piece()

    @pl.when(s % 2 == 0)
    def _():
        step(raw_a, raw_b)

    @pl.when(s % 2 == 1)
    def _():
        step(raw_b, raw_a)


def _xbc(x2, sc, sh, g, w_xbc, cw, cb, seq):
    t, d = x2.shape
    n = w_xbc.shape[1]
    taps = cw.shape[0]
    halo = SUBLANES
    tm = _row_tile(seq, 1024)
    tn = _row_tile(n, 512)
    strip = _row_tile(tn, 256)
    rb = 32
    per_b = seq // tm
    ncol = n // tn
    ntiles = (t // tm) * ncol
    per = tm // halo
    nh = t // halo

    def row_tile(s):
        return jnp.minimum(s, ntiles - 1) // ncol

    def col_tile(s):
        return jnp.minimum(s, ntiles - 1) % ncol

    def out_map(s):
        prev = jnp.maximum(s - 1, 0)
        return (prev // ncol, prev % ncol)

    return pl.pallas_call(
        functools.partial(_xbc_kernel, taps, halo, tm, ncol, per_b, strip, rb, _row_tile(d, 256)),
        grid=(ntiles + 1,),
        in_specs=[pl.BlockSpec((halo, d), lambda s: (jnp.maximum(row_tile(s) * per - 1, 0), 0)),
                  pl.BlockSpec((tm, d), lambda s: (row_tile(s), 0)),
                  pl.BlockSpec((halo, d), lambda s: (jnp.minimum((row_tile(s) + 1) * per, nh - 1), 0)),
                  pl.BlockSpec((1, 1, d), lambda s: (row_tile(s) // per_b, 0, 0)),
                  pl.BlockSpec((1, 1, d), lambda s: (row_tile(s) // per_b, 0, 0)),
                  pl.BlockSpec((1, d), lambda s: (0, 0)),
                  pl.BlockSpec((d, tn), lambda s: (0, col_tile(s))),
                  pl.BlockSpec((taps, tn), lambda s: (0, out_map(s)[1])),
                  pl.BlockSpec((1, tn), lambda s: (0, out_map(s)[1]))],
        out_specs=pl.BlockSpec((tm, tn), out_map),
        out_shape=jax.ShapeDtypeStruct((t, n), F32),
        scratch_shapes=[pltpu.VMEM((tm + 2 * halo, d), BF16),
                        pltpu.VMEM((tm + 2 * halo, tn), F32),
                        pltpu.VMEM((tm + 2 * halo, tn), F32),
                        pltpu.VMEM((min(taps, SUBLANES) - 1, tm + 2 * halo - SUBLANES, strip), F32)],
        compiler_params=_cp("arbitrary"),
        name="xbc",
    )(x2, x2, x2, sc, sh, g, w_xbc, cw, cb.reshape(1, n))


def _ssd_kernel(reverse, n_heads, hpg, pdim, *refs):
    if reverse:
        (xs_ref, b_ref, c_ref, dt_ref, bias_ref, alog_ref, yf_ref, z_ref, gn_ref,
         o_ref, h_ref, dt_t, cs_t, wend_t, cdec_t) = refs
    else:
        (xs_ref, b_ref, c_ref, dt_ref, bias_ref, alog_ref, dskip_ref,
         o_ref, h_ref, dt_t, cs_t, wend_t, cdec_t) = refs
    ln = CHUNK
    w = hpg * pdim
    hd2 = 2 * n_heads
    nt = (((1,), (1,)), ((), ()))

    @pl.when(pl.program_id(1) == 0)
    def _():
        h_ref[...] = jnp.zeros_like(h_ref)

    ri = lax.broadcasted_iota(jnp.int32, (ln, ln), 0)
    ci = lax.broadcasted_iota(jnp.int32, (ln, ln), 1)
    mask = (ci >= ri) if reverse else (ci <= ri)
    cum = mask.astype(BF16)
    edge = 0 if reverse else ln - 1

    dt_all = _softplus(dt_ref[0] + bias_ref[...])
    a_all = dt_all * (-jnp.exp(alog_ref[...]) * LOG2E)
    a_hi = a_all.astype(BF16)
    r1 = a_all - a_hi.astype(F32)
    a_mid = r1.astype(BF16)
    a_lo = (r1 - a_mid.astype(F32)).astype(BF16)
    cs_all = _bdot(cum, a_hi) + _bdot(cum, a_mid) + _bdot(cum, a_lo)
    cs_edge = cs_all[edge:edge + 1, :]
    dt_t[...] = dt_all.T
    cs_t[...] = cs_all.T
    wend_t[...] = (dt_all * jnp.exp2(cs_edge - cs_all)).T
    cdec_t[...] = jnp.broadcast_to(jnp.exp2(cs_edge), (ln, hd2)).T

    dirbase = n_heads if reverse else 0
    for g in range(N_GROUPS):
        lanes = slice(g * w, (g + 1) * w)
        xs = xs_ref[0, :, lanes]
        x_t = xs.T
        bgb = b_ref[0, :, g * D_STATE:(g + 1) * D_STATE].astype(BF16)
        cgb = c_ref[0, :, g * D_STATE:(g + 1) * D_STATE].astype(BF16)
        cb = lax.dot_general(cgb, bgb, nt, preferred_element_type=F32)
        h_in = h_ref[g]
        yoff = lax.dot_general(cgb, h_in.astype(BF16), nt, preferred_element_type=F32)
        y_parts, xw_parts, cdec_parts = [], [], []
        for jp in range(hpg // 2):
            ms, xdts, cols = [], [], []
            for j in (2 * jp, 2 * jp + 1):
                c = dirbase + g * hpg + j
                rows = slice(j * pdim, (j + 1) * pdim)
                xdts.append((x_t[rows] * dt_t[c:c + 1, :]).astype(BF16))
                xw_parts.append((x_t[rows] * wend_t[c:c + 1, :]).astype(BF16))
                col = jnp.broadcast_to(cs_all[:, c:c + 1], (ln, ln))
                seg = col - cs_t[c:c + 1, :]
                ms.append((cb * jnp.where(mask, jnp.exp2(seg), 0.0)).astype(BF16))
                cols.append(col)
                cdec_parts.append(jnp.broadcast_to(cdec_t[c:c + 1, :], (pdim, D_STATE)))
            zero = jnp.zeros((pdim, ln), BF16)
            xdt_bd = jnp.concatenate([jnp.concatenate([xdts[0], zero], axis=1),
                                      jnp.concatenate([zero, xdts[1]], axis=1)], axis=0)
            yd = lax.dot_general(jnp.concatenate(ms, axis=1), xdt_bd, nt,
                                 preferred_element_type=F32)
            din = jnp.exp2(jnp.where(ci < pdim, cols[0], cols[1]))
            y_parts.append(yd + yoff[:, 2 * jp * pdim:(2 * jp + 2) * pdim] * din)
        y = jnp.concatenate(y_parts, axis=1)
        st = _bdot(jnp.concatenate(xw_parts, axis=0), bgb)
        h_ref[g] = h_in * jnp.concatenate(cdec_parts, axis=0) + st
        if reverse:
            y = (y + yf_ref[0, :, lanes]) * _silu(z_ref[0, :, lanes])
            y = y * lax.rsqrt(jnp.mean(y * y, axis=-1, keepdims=True) + EPS) * gn_ref[:, lanes]
            o_ref[0, :, lanes] = y.astype(o_ref.dtype)
        else:
            o_ref[0, :, lanes] = y + dskip_ref[:, lanes] * xs


def _ssd(reverse, xbc3, dt3, bias_row, alog_row, extra, n_heads, d_ssm):
    bsz, seq, _ = xbc3.shape
    hd2 = 2 * n_heads
    hpg = n_heads // N_GROUPS
    pdim = d_ssm // n_heads
    w = hpg * pdim
    nc = seq // CHUNK
    ln = CHUNK
    gn = N_GROUPS * D_STATE
    assert CHUNK == D_STATE and d_ssm % gn == 0 and 2 * pdim == CHUNK and hpg % 2 == 0

    def chunk(b, z):
        return (b, nc - 1 - z if reverse else z)

    in_specs = [pl.BlockSpec((1, ln, d_ssm), lambda b, z: (*chunk(b, z), 0)),
                pl.BlockSpec((1, ln, gn), lambda b, z: (*chunk(b, z), d_ssm // gn)),
                pl.BlockSpec((1, ln, gn), lambda b, z: (*chunk(b, z), d_ssm // gn + 1)),
                pl.BlockSpec((1, ln, hd2), lambda b, z: (*chunk(b, z), 0)),
                pl.BlockSpec((1, hd2), lambda b, z: (0, 0)),
                pl.BlockSpec((1, hd2), lambda b, z: (0, 0))]
    args = [xbc3, xbc3, xbc3, dt3, bias_row, alog_row]
    if reverse:
        y_f, proj3, gnorm = extra
        in_specs += [pl.BlockSpec((1, ln, d_ssm), lambda b, z: (*chunk(b, z), 0)),
                     pl.BlockSpec((1, ln, d_ssm), lambda b, z: (*chunk(b, z), 0)),
                     pl.BlockSpec((1, d_ssm), lambda b, z: (0, 0))]
        args += [y_f, proj3, gnorm]
        out_dtype = BF16
    else:
        (dskip,) = extra
        in_specs += [pl.BlockSpec((1, d_ssm), lambda b, z: (0, 0))]
        args += [dskip]
        out_dtype = F32
    table = pltpu.VMEM((hd2, ln), F32)
    return pl.pallas_call(
        functools.partial(_ssd_kernel, reverse, n_heads, hpg, pdim),
        grid=(bsz, nc),
        in_specs=in_specs,
        out_specs=pl.BlockSpec((1, ln, d_ssm), lambda b, z: (*chunk(b, z), 0)),
        out_shape=jax.ShapeDtypeStruct((bsz, seq, d_ssm), out_dtype),
        scratch_shapes=[pltpu.VMEM((N_GROUPS, w, D_STATE), F32), table, table, table, table],
        compiler_params=_cp("arbitrary", "arbitrary"),
        name="ssd_bwd" if reverse else "ssd_fwd",
    )(*args)


def _conf_kernel(taps, halo, ts, rb, lc, pa_ref, ma_ref, na_ref, pb_ref, mb_ref, nb_ref,
                 bga_ref, bgb_ref, w_ref, bdw_ref, lng_ref, lnb_ref, o_ref, ext_ref, shift_ref, conv_ref):
    d = o_ref.shape[-1]

    def glu(a, b):
        return (a + bga_ref[...]) * jax.nn.sigmoid(b + bgb_ref[...])

    _fill_ext(ext_ref, glu(pa_ref[0], pb_ref[0]), glu(ma_ref[0], mb_ref[0]),
              glu(na_ref[0], nb_ref[0]), halo, ts)
    nrows = shift_ref.shape[1]
    for q in range(SUBLANES):
        shift_ref[q] = ext_ref[q:q + nrows, :]
    first = halo - (taps - 1) // 2

    for c0 in range(0, d, lc):
        lanes = slice(c0, c0 + lc)
        nblk = ts // SUBLANES
        acc = [jnp.broadcast_to(bdw_ref[:, lanes], (SUBLANES, lc))] * nblk
        for q in range(SUBLANES):
            ms = [(first + k) // SUBLANES for k in range(taps) if (first + k) % SUBLANES == q]
            w_b = {m: jnp.broadcast_to(w_ref[SUBLANES * m + q - first:SUBLANES * m + q - first + 1, lanes],
                                       (SUBLANES, lc)) for m in ms}
            for i in range(nblk + max(ms)):
                blk = shift_ref[q, SUBLANES * i:SUBLANES * (i + 1), lanes]
                for m in ms:
                    if 0 <= i - m < nblk:
                        acc[i - m] = acc[i - m] + blk * w_b[m]
        for i in range(nblk):
            conv_ref[SUBLANES * i:SUBLANES * (i + 1), lanes] = acc[i]

    u = conv_ref[...]
    uc = u - jnp.mean(u, axis=-1, keepdims=True)
    y = uc * lax.rsqrt(jnp.mean(uc * uc, axis=-1, keepdims=True) + EPS)
    y = y * lng_ref[...] + lnb_ref[...]
    o_ref[0] = _silu(y).astype(o_ref.dtype)


def _conf(proj3, col_off, b_glu, w_dw, b_dw, ln_g, ln_b):
    bsz, seq, _ = proj3.shape
    taps, d = w_dw.shape
    halo = 16
    ts = _row_tile(seq, 256)
    rb = 64
    lc = 128
    ablk = col_off // d
    specs_a = _halo_specs(ts, halo, seq, d, lambda: ablk)
    specs_b = _halo_specs(ts, halo, seq, d, lambda: ablk + 1)
    row = lambda b_, s: (0, 0)
    return pl.pallas_call(
        functools.partial(_conf_kernel, taps, halo, ts, rb, lc),
        grid=(bsz, seq // ts),
        in_specs=specs_a + specs_b + [pl.BlockSpec((1, d), row), pl.BlockSpec((1, d), row),
                                      pl.BlockSpec((taps, d), row), pl.BlockSpec((1, d), row),
                                      pl.BlockSpec((1, d), row), pl.BlockSpec((1, d), row)],
        out_specs=pl.BlockSpec((1, ts, d), lambda b_, s: (b_, s, 0)),
        out_shape=jax.ShapeDtypeStruct((bsz, seq, d), BF16),
        scratch_shapes=[pltpu.VMEM((ts + 2 * halo, d), F32),
                        pltpu.VMEM((SUBLANES, ts + 2 * halo - SUBLANES, d), F32),
                        pltpu.VMEM((ts, d), F32)],
        compiler_params=_cp("arbitrary", "arbitrary"),
        name="conf",
    )(proj3, proj3, proj3, proj3, proj3, proj3,
      b_glu[:d].reshape(1, d), b_glu[d:].reshape(1, d), w_dw, b_dw.reshape(1, d),
      ln_g.reshape(1, d), ln_b.reshape(1, d))


def _merge_kernel(ya_ref, u_ref, wa_ref, wb_ref, bb_ref, ga_ref, gb_ref, bga_ref, bgb_ref, o_ref):
    y_a = _bdot(ya_ref[...], wa_ref[...])
    y_b = _bdot(u_ref[...], wb_ref[...]) + bb_ref[...]
    g_a = jax.nn.sigmoid(ga_ref[...] + bga_ref[...])
    g_b = jax.nn.sigmoid(gb_ref[...] + bgb_ref[...])
    o_ref[...] = (g_a * y_a + g_b * y_b).astype(o_ref.dtype)


def _merge(y_ssd, u, w_ssm_out, w_conv_out, b_conv_out, proj2, gate_off, b_gate):
    t, d_ssm = y_ssd.shape
    d_conv = u.shape[1]
    d = w_ssm_out.shape[1]
    tm = _row_tile(t, 1024)
    tn = _row_tile(d, 512)
    ga = gate_off // tn
    gb = (gate_off + d) // tn
    bg = b_gate.reshape(1, 2 * d)
    return pl.pallas_call(
        _merge_kernel,
        grid=(t // tm, d // tn),
        in_specs=[pl.BlockSpec((tm, d_ssm), lambda i, j: (i, 0)),
                  pl.BlockSpec((tm, d_conv), lambda i, j: (i, 0)),
                  pl.BlockSpec((d_ssm, tn), lambda i, j: (0, j)),
                  pl.BlockSpec((d_conv, tn), lambda i, j: (0, j)),
                  pl.BlockSpec((1, tn), lambda i, j: (0, j)),
                  pl.BlockSpec((tm, tn), lambda i, j: (i, ga + j)),
                  pl.BlockSpec((tm, tn), lambda i, j: (i, gb + j)),
                  pl.BlockSpec((1, tn), lambda i, j: (0, j)),
                  pl.BlockSpec((1, tn), lambda i, j: (0, d // tn + j))],
        out_specs=pl.BlockSpec((tm, tn), lambda i, j: (i, j)),
        out_shape=jax.ShapeDtypeStruct((t, d), BF16),
        compiler_params=_cp("arbitrary", "arbitrary"),
        name="merge",
    )(y_ssd, u, w_ssm_out, w_conv_out, b_conv_out.reshape(1, d), proj2, proj2, bg, bg)


def _mixout_kernel(m_ref, w_ref, x_ref, gpost_ref, g1_ref, gpre_ref, sc2_ref, sh2_ref, x1_ref, h2_ref):
    mix = _bdot(m_ref[...], w_ref[...])
    nm = mix * lax.rsqrt(jnp.mean(mix * mix, axis=-1, keepdims=True) + EPS) * gpost_ref[...]
    x1 = x_ref[...] + g1_ref[0] * nm
    x1_ref[...] = x1
    y = x1 * lax.rsqrt(jnp.mean(x1 * x1, axis=-1, keepdims=True) + EPS) * gpre_ref[...]
    h2_ref[...] = (y * (1.0 + sc2_ref[0]) + sh2_ref[0]).astype(h2_ref.dtype)


def _mixout(mix_in, w_mix_out, x2, g_post, g1, g_pre_ffn, sc2, sh2, seq):
    t, d = x2.shape
    tm = _row_tile(seq, 512)
    per_b = seq // tm
    row = lambda i: (0, 0)
    brow = lambda i: (i // per_b, 0, 0)
    return pl.pallas_call(
        _mixout_kernel,
        grid=(t // tm,),
        in_specs=[pl.BlockSpec((tm, d), lambda i: (i, 0)),
                  pl.BlockSpec((d, d), row),
                  pl.BlockSpec((tm, d), lambda i: (i, 0)),
                  pl.BlockSpec((1, d), row),
                  pl.BlockSpec((1, 1, d), brow),
                  pl.BlockSpec((1, d), row),
                  pl.BlockSpec((1, 1, d), brow),
                  pl.BlockSpec((1, 1, d), brow)],
        out_specs=[pl.BlockSpec((tm, d), lambda i: (i, 0)),
                   pl.BlockSpec((tm, d), lambda i: (i, 0))],
        out_shape=[jax.ShapeDtypeStruct((t, d), F32), jax.ShapeDtypeStruct((t, d), BF16)],
        compiler_params=_cp("arbitrary"),
        name="mixout",
    )(mix_in, w_mix_out, x2, g_post, g1, g_pre_ffn, sc2, sh2)


def _ffn_up_kernel(h_ref, wg_ref, wu_ref, o_ref):
    h = h_ref[...]
    o_ref[...] = (_silu(_bdot(h, wg_ref[...])) * _bdot(h, wu_ref[...])).astype(o_ref.dtype)


def _ffn_up(h2, w_gate_up):
    t, d = h2.shape
    d_ff = w_gate_up.shape[1] // 2
    tm = _row_tile(t, 1024)
    tn = _row_tile(d_ff, 512)
    nj = d_ff // tn
    return pl.pallas_call(
        _ffn_up_kernel,
        grid=(t // tm, nj),
        in_specs=[pl.BlockSpec((tm, d), lambda i, j: (i, 0)),
                  pl.BlockSpec((d, tn), lambda i, j: (0, j)),
                  pl.BlockSpec((d, tn), lambda i, j: (0, nj + j))],
        out_specs=pl.BlockSpec((tm, tn), lambda i, j: (i, j)),
        out_shape=jax.ShapeDtypeStruct((t, d_ff), BF16),
        compiler_params=_cp("arbitrary", "arbitrary"),
        name="ffn_up",
    )(h2, w_gate_up, w_gate_up)


def _ffn_down_kernel(a_ref, w_ref, x1_ref, gpost_ref, g2_ref, o_ref):
    k = pl.program_id(1)
    part = _bdot(a_ref[...], w_ref[...])

    @pl.when(k == 0)
    def _():
        o_ref[...] = part

    @pl.when(k > 0)
    def _():
        o_ref[...] += part

    @pl.when(k == pl.num_programs(1) - 1)
    def _():
        f = o_ref[...]
        nf = f * lax.rsqrt(jnp.mean(f * f, axis=-1, keepdims=True) + EPS) * gpost_ref[...]
        o_ref[...] = x1_ref[...] + g2_ref[0] * nf


def _ffn_down(act, w_down, x1, g_post, g2, seq):
    t, d_ff = act.shape
    d = w_down.shape[1]
    tm = _row_tile(seq, 512)
    tk = _lane_tile(d_ff, 2816)
    per_b = seq // tm
    return pl.pallas_call(
        _ffn_down_kernel,
        grid=(t // tm, d_ff // tk),
        in_specs=[pl.BlockSpec((tm, tk), lambda i, k: (i, k)),
                  pl.BlockSpec((tk, d), lambda i, k: (k, 0)),
                  pl.BlockSpec((tm, d), lambda i, k: (i, 0), pipeline_mode=pl.Buffered(1)),
                  pl.BlockSpec((1, d), lambda i, k: (0, 0)),
                  pl.BlockSpec((1, 1, d), lambda i, k: (i // per_b, 0, 0))],
        out_specs=pl.BlockSpec((tm, d), lambda i, k: (i, 0)),
        out_shape=jax.ShapeDtypeStruct((t, d), F32),
        compiler_params=_cp("arbitrary", "arbitrary"),
        name="ffn_down",
    )(act, w_down, x1, g_post, g2)


def kernel(x, c, w_ada, b_ada, g_pre_mix, g_post_mix, w_in, w_conv_ssm, b_conv_ssm, dt_bias_fwd, dt_bias_bwd, a_log_fwd, a_log_bwd, d_skip, g_ssm_norm, w_ssm_out, b_glu, w_dw, b_dw, ln_g, ln_b, w_conv_out, b_conv_out, b_gate, w_mix_out, g_pre_ffn, g_post_ffn, w_gate_up, w_down):
    bsz, seq, d = x.shape
    depth = w_ada.shape[0]
    n_heads = dt_bias_fwd.shape[1]
    d_ssm = w_ssm_out.shape[1]
    d_xbc = w_conv_ssm.shape[2]
    d_conv = w_dw.shape[2]
    pdim = d_ssm // n_heads
    assert d_xbc == d_ssm + 2 * N_GROUPS * D_STATE and seq % CHUNK == 0 and n_heads % N_GROUPS == 0
    s1, s2, s3 = d_ssm, d_ssm + d_xbc, d_ssm + d_xbc + 2 * n_heads
    glu_off, gate_off = d_ssm, d_ssm + 2 * d_conv

    x2 = x.reshape(bsz * seq, d)
    for l in range(depth):
        w_rest = jnp.concatenate([w_in[l][:, :s1], w_in[l][:, s3:]], axis=1).astype(BF16)
        w_xbc = w_in[l][:, s1:s2].astype(BF16)
        w_dt = w_in[l][:, s2:s3].astype(BF16)
        row = lambda v: v.reshape(1, -1)

        mod = _ada(c, w_ada[l], b_ada[l])
        sh1, sc1, g1, sh2, sc2, g2 = [m.reshape(bsz, 1, d) for m in jnp.split(mod, 6, axis=-1)]

        proj2, dt2 = _inproj(x2, sc1, sh1, row(g_pre_mix[l]), w_rest, w_dt, seq)
        proj3 = proj2.reshape(bsz, seq, -1)
        dt3 = dt2.reshape(bsz, seq, -1)

        xbc3 = _xbc(x2, sc1, sh1, row(g_pre_mix[l]), w_xbc, w_conv_ssm[l], b_conv_ssm[l],
                    seq).reshape(bsz, seq, d_xbc)
        bias_row = row(jnp.concatenate([dt_bias_fwd[l], dt_bias_bwd[l]]))
        alog_row = row(jnp.concatenate([a_log_fwd[l], a_log_bwd[l]]))
        dskip_row = row(jnp.repeat(d_skip[l], pdim))
        y_f = _ssd(False, xbc3, dt3, bias_row, alog_row, (dskip_row,), n_heads, d_ssm)
        y_ssd = _ssd(True, xbc3, dt3, bias_row, alog_row, (y_f, proj3, row(g_ssm_norm[l])), n_heads, d_ssm)

        u = _conf(proj3, glu_off, b_glu[l], w_dw[l], b_dw[l], ln_g[l], ln_b[l])

        mix_in = _merge(y_ssd.reshape(bsz * seq, d_ssm), u.reshape(bsz * seq, d_conv),
                        w_ssm_out[l].astype(BF16), w_conv_out[l].astype(BF16), b_conv_out[l],
                        proj2, gate_off, b_gate[l])
        x1, h2 = _mixout(mix_in, w_mix_out[l].astype(BF16), x2, row(g_post_mix[l]), g1,
                         row(g_pre_ffn[l]), sc2, sh2, seq)
        act = _ffn_up(h2, w_gate_up[l].astype(BF16))
        x2 = _ffn_down(act, w_down[l].astype(BF16), x1, row(g_post_ffn[l]), g2, seq)
    return x2.reshape(bsz, seq, d)
```

```python
import functools

import jax
import jax.numpy as jnp
from jax import lax
from jax.experimental import pallas as pl
from jax.experimental.pallas import tpu as pltpu

F32 = jnp.float32
BF16 = jnp.bfloat16

EPS = 1e-6
N_GROUPS = 8
D_STATE = 128
CHUNK = 128
V7X_VMEM_LIMIT_BYTES = 58 * 1024 * 1024
SUBLANES = 8
LANES = 128
LOG2E = 1.4426950408889634


def _cp(*sem):
    return pltpu.CompilerParams(dimension_semantics=sem, vmem_limit_bytes=V7X_VMEM_LIMIT_BYTES)


def _silu(v):
    return v * jax.nn.sigmoid(v)


def _softplus(v):
    return jnp.maximum(v, 0.0) + jnp.log1p(jnp.exp(-jnp.abs(v)))


def _bdot(a, b):
    return jnp.dot(a, b, preferred_element_type=F32)


def _row_tile(n, want):
    t = min(n, want)
    assert n % t == 0, (n, t)
    return t


def _lane_tile(n, want):
    return max(t for t in range(LANES, min(n, want) + 1, LANES) if n % t == 0)


def _ada_kernel(c_ref, w_ref, b_ref, o_ref):
    ca = _silu(c_ref[...]).astype(BF16)
    o_ref[...] = _bdot(ca, w_ref[...].astype(BF16)) + b_ref[...]


def _ada(c, w, b):
    bsz, d = c.shape
    n = w.shape[1]
    rows = 16
    tn = _row_tile(n, 1024)
    cpad = jnp.zeros((rows, d), F32).at[:bsz].set(c)
    out = pl.pallas_call(
        _ada_kernel,
        grid=(n // tn,),
        in_specs=[pl.BlockSpec((rows, d), lambda j: (0, 0)),
                  pl.BlockSpec((d, tn), lambda j: (0, j)),
                  pl.BlockSpec((1, tn), lambda j: (0, j))],
        out_specs=pl.BlockSpec((rows, tn), lambda j: (0, j)),
        out_shape=jax.ShapeDtypeStruct((rows, n), F32),
        compiler_params=_cp("arbitrary"),
        name="ada",
    )(cpad, w, b.reshape(1, n))
    return out[:bsz]


def _inproj_kernel(x_ref, sc_ref, sh_ref, g_ref, w_ref, wdt_ref, o_ref, odt_ref, h_ref):
    @pl.when(pl.program_id(1) == 0)
    def _():
        x = x_ref[...]
        y = x * lax.rsqrt(jnp.mean(x * x, axis=-1, keepdims=True) + EPS) * g_ref[...]
        h = (y * (1.0 + sc_ref[0]) + sh_ref[0]).astype(BF16)
        h_ref[...] = h
        odt_ref[...] = _bdot(h, wdt_ref[...])

    o_ref[...] = _bdot(h_ref[...], w_ref[...])


def _inproj(x2, sc, sh, g, w_main, w_dt, seq):
    t, d = x2.shape
    n = w_main.shape[1]
    ndt = w_dt.shape[1]
    tm = _row_tile(seq, 1024)
    tn = _row_tile(n, 1024)
    per_b = seq // tm
    return pl.pallas_call(
        _inproj_kernel,
        grid=(t // tm, n // tn),
        in_specs=[pl.BlockSpec((tm, d), lambda i, j: (i, 0)),
                  pl.BlockSpec((1, 1, d), lambda i, j: (i // per_b, 0, 0)),
                  pl.BlockSpec((1, 1, d), lambda i, j: (i // per_b, 0, 0)),
                  pl.BlockSpec((1, d), lambda i, j: (0, 0)),
                  pl.BlockSpec((d, tn), lambda i, j: (0, j)),
                  pl.BlockSpec((d, ndt), lambda i, j: (0, 0))],
        out_specs=[pl.BlockSpec((tm, tn), lambda i, j: (i, j)),
                   pl.BlockSpec((tm, ndt), lambda i, j: (i, 0))],
        out_shape=[jax.ShapeDtypeStruct((t, n), F32),
                   jax.ShapeDtypeStruct((t, ndt), F32)],
        scratch_shapes=[pltpu.VMEM((tm, d), BF16)],
        compiler_params=_cp("arbitrary", "arbitrary"),
        name="inproj",
    )(x2, sc, sh, g, w_main, w_dt)


def _fill_ext(ext_ref, prev, main, nxt, halo, ts):
    s = pl.program_id(1)
    last = pl.num_programs(1) - 1
    ext_ref[0:halo, :] = jnp.where(s > 0, prev, 0.0)
    ext_ref[halo:halo + ts, :] = main
    ext_ref[halo + ts:halo + ts + halo, :] = jnp.where(s < last, nxt, 0.0)


def _dwconv_rows(tap_rows, w_ref, b_ref, rb, lanes, taps):
    acc = jnp.broadcast_to(b_ref[:, lanes], (rb, lanes.stop - lanes.start))
    for k in range(taps):
        acc = acc + tap_rows(k) * w_ref[k:k + 1, lanes]
    return acc


def _halo_specs(ts, halo, seq, width, col_block):
    per = ts // halo
    nh = seq // halo

    def prev_map(b, s, *c):
        return (b, jnp.maximum(s * per - 1, 0), col_block(*c))

    def main_map(b, s, *c):
        return (b, s, col_block(*c))

    def next_map(b, s, *c):
        return (b, jnp.minimum((s + 1) * per, nh - 1), col_block(*c))

    return [pl.BlockSpec((1, halo, width), prev_map),
            pl.BlockSpec((1, ts, width), main_map),
            pl.BlockSpec((1, halo, width), next_map)]


def _xbc_kernel(taps, halo, tm, ncol, per_b, strip, rb, kc,
                xp_ref, xm_ref, xn_ref, sc_ref, sh_ref, g_ref, w_ref, cw_ref, cb_ref,
                o_ref, h_ref, raw_a, raw_b, shift_ref):
    s = pl.program_id(0)
    ntiles = pl.num_programs(0) - 1
    cur = jnp.minimum(s, ntiles - 1)
    tn = o_ref.shape[-1]
    first = halo - (taps - 1) // 2
    shifts = sorted({(first + k) % SUBLANES for k in range(taps)} - {0})
    nrows = shift_ref.shape[1]

    @pl.when(s == 0)
    def _():
        raw_b[...] = jnp.zeros_like(raw_b)

    @pl.when((s < ntiles) & (cur % ncol == 0))
    def _():
        def modulated(x):
            y = x * lax.rsqrt(jnp.mean(x * x, axis=-1, keepdims=True) + EPS) * g_ref[...]
            return y * (1.0 + sc_ref[0]) + sh_ref[0]

        h_ref[...] = jnp.concatenate(
            [modulated(xp_ref[...]), modulated(xm_ref[...]), modulated(xn_ref[...])], axis=0).astype(BF16)

    def step(raw_w, raw_r):
        pos = (jnp.maximum(s - 1, 0) // ncol) % per_b
        raw_r[0:halo, :] = jnp.where(pos > 0, raw_r[0:halo, :], 0.0)
        raw_r[halo + tm:, :] = jnp.where(pos < per_b - 1, raw_r[halo + tm:, :], 0.0)
        pieces = []
        for c0 in range(0, tn, strip):
            lanes = slice(c0, c0 + strip)

            def copy_shifts(lanes=lanes):
                for n, q in enumerate(shifts):
                    shift_ref[n] = raw_r[q:q + nrows, lanes]

            def tap_rows(r0, k, lanes=lanes):
                m, q = divmod(first + k, SUBLANES)
                rows = slice(r0 + SUBLANES * m, r0 + SUBLANES * m + rb)
                return raw_r[rows, lanes] if q == 0 else shift_ref[shifts.index(q), rows, :]

            def conv_rows(r0, lanes=lanes, tap_rows=tap_rows):
                acc = _dwconv_rows(functools.partial(tap_rows, r0), cw_ref, cb_ref, rb, lanes, taps)
                o_ref[r0:r0 + rb, lanes] = _silu(acc)

            pieces.append(copy_shifts)
            pieces += [functools.partial(conv_rows, r0) for r0 in range(0, tm, rb)]
        nk = h_ref.shape[1] // kc
        per_chunk = -(-len(pieces) // nk)
        for c in range(nk):
            part = _bdot(h_ref[:, c * kc:(c + 1) * kc], w_ref[c * kc:(c + 1) * kc, :])
            if c == 0:
                raw_w[...] = part
            else:
                raw_w[...] += part
            for piece in pieces[c * per_chunk:(c + 1) * per_chunk]:
                piece()

    @pl.when(s % 2 == 0)
    def _():
        step(raw_a, raw_b)

    @pl.when(s % 2 == 1)
    def _():
        step(raw_b, raw_a)


def _xbc(x2, sc, sh, g, w_xbc, cw, cb, seq):
    t, d = x2.shape
    n = w_xbc.shape[1]
    taps = cw.shape[0]
    halo = SUBLANES
    tm = _row_tile(seq, 1024)
    tn = _row_tile(n, 512)
    strip = _row_tile(tn, 256)
    rb = 32
    per_b = seq // tm
    ncol = n // tn
    ntiles = (t // tm) * ncol
    per = tm // halo
    nh = t // halo

    def row_tile(s):
        return jnp.minimum(s, ntiles - 1) // ncol

    def col_tile(s):
        return jnp.minimum(s, ntiles - 1) % ncol

    def out_map(s):
        prev = jnp.maximum(s - 1, 0)
        return (prev // ncol, prev % ncol)

    return pl.pallas_call(
        functools.partial(_xbc_kernel, taps, halo, tm, ncol, per_b, strip, rb, _row_tile(d, 1024)),
        grid=(ntiles + 1,),
        in_specs=[pl.BlockSpec((halo, d), lambda s: (jnp.maximum(row_tile(s) * per - 1, 0), 0)),
                  pl.BlockSpec((tm, d), lambda s: (row_tile(s), 0)),
                  pl.BlockSpec((halo, d), lambda s: (jnp.minimum((row_tile(s) + 1) * per, nh - 1), 0)),
                  pl.BlockSpec((1, 1, d), lambda s: (row_tile(s) // per_b, 0, 0)),
                  pl.BlockSpec((1, 1, d), lambda s: (row_tile(s) // per_b, 0, 0)),
                  pl.BlockSpec((1, d), lambda s: (0, 0)),
                  pl.BlockSpec((d, tn), lambda s: (0, col_tile(s))),
                  pl.BlockSpec((taps, tn), lambda s: (0, out_map(s)[1])),
                  pl.BlockSpec((1, tn), lambda s: (0, out_map(s)[1]))],
        out_specs=pl.BlockSpec((tm, tn), out_map),
        out_shape=jax.ShapeDtypeStruct((t, n), F32),
        scratch_shapes=[pltpu.VMEM((tm + 2 * halo, d), BF16),
                        pltpu.VMEM((tm + 2 * halo, tn), F32),
                        pltpu.VMEM((tm + 2 * halo, tn), F32),
                        pltpu.VMEM((min(taps, SUBLANES) - 1, tm + 2 * halo - SUBLANES, strip), F32)],
        compiler_params=_cp("arbitrary"),
        name="xbc",
    )(x2, x2, x2, sc, sh, g, w_xbc, cw, cb.reshape(1, n))


def _ssd_kernel(reverse, n_heads, hpg, pdim, *refs):
    if reverse:
        (xs_ref, b_ref, c_ref, dt_ref, bias_ref, alog_ref, yf_ref, z_ref, gn_ref,
         o_ref, h_ref, dt_t, cs_t, wend_t, cdec_t) = refs
    else:
        (xs_ref, b_ref, c_ref, dt_ref, bias_ref, alog_ref, dskip_ref,
         o_ref, h_ref, dt_t, cs_t, wend_t, cdec_t) = refs
    ln = CHUNK
    w = hpg * pdim
    hd2 = 2 * n_heads
    nt = (((1,), (1,)), ((), ()))

    @pl.when(pl.program_id(1) == 0)
    def _():
        h_ref[...] = jnp.zeros_like(h_ref)

    ri = lax.broadcasted_iota(jnp.int32, (ln, ln), 0)
    ci = lax.broadcasted_iota(jnp.int32, (ln, ln), 1)
    mask = (ci >= ri) if reverse else (ci <= ri)
    cum = mask.astype(BF16)
    edge = 0 if reverse else ln - 1

    dt_all = _softplus(dt_ref[0] + bias_ref[...])
    a_all = dt_all * (-jnp.exp(alog_ref[...]) * LOG2E)
    a_hi = a_all.astype(BF16)
    r1 = a_all - a_hi.astype(F32)
    a_mid = r1.astype(BF16)
    a_lo = (r1 - a_mid.astype(F32)).astype(BF16)
    cs_all = _bdot(cum, a_hi) + _bdot(cum, a_mid) + _bdot(cum, a_lo)
    cs_edge = cs_all[edge:edge + 1, :]
    dt_t[...] = dt_all.T
    cs_t[...] = cs_all.T
    wend_t[...] = (dt_all * jnp.exp2(cs_edge - cs_all)).T
    cdec_t[...] = jnp.broadcast_to(jnp.exp2(cs_edge), (ln, hd2)).T

    dirbase = n_heads if reverse else 0
    for g in range(N_GROUPS):
        lanes = slice(g * w, (g + 1) * w)
        xs = xs_ref[0, :, lanes]
        x_t = xs.T
        bgb = b_ref[0, :, g * D_STATE:(g + 1) * D_STATE].astype(BF16)
        cgb = c_ref[0, :, g * D_STATE:(g + 1) * D_STATE].astype(BF16)
        cb = lax.dot_general(cgb, bgb, nt, preferred_element_type=F32)
        h_in = h_ref[g]
        yoff = lax.dot_general(cgb, h_in.astype(BF16), nt, preferred_element_type=F32)
        y_parts, xw_parts, cdec_parts = [], [], []
        for jp in range(hpg // 2):
            ms, xdts, cols = [], [], []
            for j in (2 * jp, 2 * jp + 1):
                c = dirbase + g * hpg + j
                rows = slice(j * pdim, (j + 1) * pdim)
                xdts.append((x_t[rows] * dt_t[c:c + 1, :]).astype(BF16))
                xw_parts.append((x_t[rows] * wend_t[c:c + 1, :]).astype(BF16))
                col = jnp.broadcast_to(cs_all[:, c:c + 1], (ln, ln))
                seg = col - cs_t[c:c + 1, :]
                ms.append((cb * jnp.where(mask, jnp.exp2(seg), 0.0)).astype(BF16))
                cols.append(col)
                cdec_parts.append(jnp.broadcast_to(cdec_t[c:c + 1, :], (pdim, D_STATE)))
            zero = jnp.zeros((pdim, ln), BF16)
            xdt_bd = jnp.concatenate([jnp.concatenate([xdts[0], zero], axis=1),
                                      jnp.concatenate([zero, xdts[1]], axis=1)], axis=0)
            yd = lax.dot_general(jnp.concatenate(ms, axis=1), xdt_bd, nt,
                                 preferred_element_type=F32)
            din = jnp.exp2(jnp.where(ci < pdim, cols[0], cols[1]))
            y_parts.append(yd + yoff[:, 2 * jp * pdim:(2 * jp + 2) * pdim] * din)
        y = jnp.concatenate(y_parts, axis=1)
        st = _bdot(jnp.concatenate(xw_parts, axis=0), bgb)
        h_ref[g] = h_in * jnp.concatenate(cdec_parts, axis=0) + st
        if reverse:
            y = (y + yf_ref[0, :, lanes]) * _silu(z_ref[0, :, lanes])
            y = y * lax.rsqrt(jnp.mean(y * y, axis=-1, keepdims=True) + EPS) * gn_ref[:, lanes]
            o_ref[0, :, lanes] = y.astype(o_ref.dtype)
        else:
            o_ref[0, :, lanes] = y + dskip_ref[:, lanes] * xs


def _ssd(reverse, xbc3, dt3, bias_row, alog_row, extra, n_heads, d_ssm):
    bsz, seq, _ = xbc3.shape
    hd2 = 2 * n_heads
    hpg = n_heads // N_GROUPS
    pdim = d_ssm // n_heads
    w = hpg * pdim
    nc = seq // CHUNK
    ln = CHUNK
    gn = N_GROUPS * D_STATE
    assert CHUNK == D_STATE and d_ssm % gn == 0 and 2 * pdim == CHUNK and hpg % 2 == 0

    def chunk(b, z):
        return (b, nc - 1 - z if reverse else z)

    in_specs = [pl.BlockSpec((1, ln, d_ssm), lambda b, z: (*chunk(b, z), 0)),
                pl.BlockSpec((1, ln, gn), lambda b, z: (*chunk(b, z), d_ssm // gn)),
                pl.BlockSpec((1, ln, gn), lambda b, z: (*chunk(b, z), d_ssm // gn + 1)),
                pl.BlockSpec((1, ln, hd2), lambda b, z: (*chunk(b, z), 0)),
                pl.BlockSpec((1, hd2), lambda b, z: (0, 0)),
                pl.BlockSpec((1, hd2), lambda b, z: (0, 0))]
    args = [xbc3, xbc3, xbc3, dt3, bias_row, alog_row]
    if reverse:
        y_f, proj3, gnorm = extra
        in_specs += [pl.BlockSpec((1, ln, d_ssm), lambda b, z: (*chunk(b, z), 0)),
                     pl.BlockSpec((1, ln, d_ssm), lambda b, z: (*chunk(b, z), 0)),
                     pl.BlockSpec((1, d_ssm), lambda b, z: (0, 0))]
        args += [y_f, proj3, gnorm]
        out_dtype = BF16
    else:
        (dskip,) = extra
        in_specs += [pl.BlockSpec((1, d_ssm), lambda b, z: (0, 0))]
        args += [dskip]
        out_dtype = F32
    table = pltpu.VMEM((hd2, ln), F32)
    return pl.pallas_call(
        functools.partial(_ssd_kernel, reverse, n_heads, hpg, pdim),
        grid=(bsz, nc),
        in_specs=in_specs,
        out_specs=pl.BlockSpec((1, ln, d_ssm), lambda b, z: (*chunk(b, z), 0)),
        out_shape=jax.ShapeDtypeStruct((bsz, seq, d_ssm), out_dtype),
        scratch_shapes=[pltpu.VMEM((N_GROUPS, w, D_STATE), F32), table, table, table, table],
        compiler_params=_cp("arbitrary", "arbitrary"),
        name="ssd_bwd" if reverse else "ssd_fwd",
    )(*args)


def _conf_kernel(taps, halo, ts, rb, lc, pa_ref, ma_ref, na_ref, pb_ref, mb_ref, nb_ref,
                 bga_ref, bgb_ref, w_ref, bdw_ref, lng_ref, lnb_ref, o_ref, ext_ref, shift_ref, conv_ref):
    d = o_ref.shape[-1]

    def glu(a, b):
        return (a + bga_ref[...]) * jax.nn.sigmoid(b + bgb_ref[...])

    _fill_ext(ext_ref, glu(pa_ref[0], pb_ref[0]), glu(ma_ref[0], mb_ref[0]),
              glu(na_ref[0], nb_ref[0]), halo, ts)
    nrows = shift_ref.shape[1]
    for q in range(SUBLANES):
        shift_ref[q] = ext_ref[q:q + nrows, :]
    first = halo - (taps - 1) // 2

    for c0 in range(0, d, lc):
        lanes = slice(c0, c0 + lc)
        nblk = ts // SUBLANES
        acc = [jnp.broadcast_to(bdw_ref[:, lanes], (SUBLANES, lc))] * nblk
        for q in range(SUBLANES):
            ms = [(first + k) // SUBLANES for k in range(taps) if (first + k) % SUBLANES == q]
            w_b = {m: jnp.broadcast_to(w_ref[SUBLANES * m + q - first:SUBLANES * m + q - first + 1, lanes],
                                       (SUBLANES, lc)) for m in ms}
            for i in range(nblk + max(ms)):
                blk = shift_ref[q, SUBLANES * i:SUBLANES * (i + 1), lanes]
                for m in ms:
                    if 0 <= i - m < nblk:
                        acc[i - m] = acc[i - m] + blk * w_b[m]
        for i in range(nblk):
            conv_ref[SUBLANES * i:SUBLANES * (i + 1), lanes] = acc[i]

    u = conv_ref[...]
    uc = u - jnp.mean(u, axis=-1, keepdims=True)
    y = uc * lax.rsqrt(jnp.mean(uc * uc, axis=-1, keepdims=True) + EPS)
    y = y * lng_ref[...] + lnb_ref[...]
    o_ref[0] = _silu(y).astype(o_ref.dtype)


def _conf(proj3, col_off, b_glu, w_dw, b_dw, ln_g, ln_b):
    bsz, seq, _ = proj3.shape
    taps, d = w_dw.shape
    halo = 16
    ts = _row_tile(seq, 256)
    rb = 64
    lc = 128
    ablk = col_off // d
    specs_a = _halo_specs(ts, halo, seq, d, lambda: ablk)
    specs_b = _halo_specs(ts, halo, seq, d, lambda: ablk + 1)
    row = lambda b_, s: (0, 0)
    return pl.pallas_call(
        functools.partial(_conf_kernel, taps, halo, ts, rb, lc),
        grid=(bsz, seq // ts),
        in_specs=specs_a + specs_b + [pl.BlockSpec((1, d), row), pl.BlockSpec((1, d), row),
                                      pl.BlockSpec((taps, d), row), pl.BlockSpec((1, d), row),
                                      pl.BlockSpec((1, d), row), pl.BlockSpec((1, d), row)],
        out_specs=pl.BlockSpec((1, ts, d), lambda b_, s: (b_, s, 0)),
        out_shape=jax.ShapeDtypeStruct((bsz, seq, d), BF16),
        scratch_shapes=[pltpu.VMEM((ts + 2 * halo, d), F32),
                        pltpu.VMEM((SUBLANES, ts + 2 * halo - SUBLANES, d), F32),
                        pltpu.VMEM((ts, d), F32)],
        compiler_params=_cp("arbitrary", "arbitrary"),
        name="conf",
    )(proj3, proj3, proj3, proj3, proj3, proj3,
      b_glu[:d].reshape(1, d), b_glu[d:].reshape(1, d), w_dw, b_dw.reshape(1, d),
      ln_g.reshape(1, d), ln_b.reshape(1, d))


def _merge_kernel(ya_ref, u_ref, wa_ref, wb_ref, bb_ref, ga_ref, gb_ref, bga_ref, bgb_ref, o_ref):
    y_a = _bdot(ya_ref[...], wa_ref[...])
    y_b = _bdot(u_ref[...], wb_ref[...]) + bb_ref[...]
    g_a = jax.nn.sigmoid(ga_ref[...] + bga_ref[...])
    g_b = jax.nn.sigmoid(gb_ref[...] + bgb_ref[...])
    o_ref[...] = (g_a * y_a + g_b * y_b).astype(o_ref.dtype)


def _merge(y_ssd, u, w_ssm_out, w_conv_out, b_conv_out, proj2, gate_off, b_gate):
    t, d_ssm = y_ssd.shape
    d_conv = u.shape[1]
    d = w_ssm_out.shape[1]
    tm = _row_tile(t, 1024)
    tn = _row_tile(d, 512)
    ga = gate_off // tn
    gb = (gate_off + d) // tn
    bg = b_gate.reshape(1, 2 * d)
    return pl.pallas_call(
        _merge_kernel,
        grid=(t // tm, d // tn),
        in_specs=[pl.BlockSpec((tm, d_ssm), lambda i, j: (i, 0)),
                  pl.BlockSpec((tm, d_conv), lambda i, j: (i, 0)),
                  pl.BlockSpec((d_ssm, tn), lambda i, j: (0, j)),
                  pl.BlockSpec((d_conv, tn), lambda i, j: (0, j)),
                  pl.BlockSpec((1, tn), lambda i, j: (0, j)),
                  pl.BlockSpec((tm, tn), lambda i, j: (i, ga + j)),
                  pl.BlockSpec((tm, tn), lambda i, j: (i, gb + j)),
                  pl.BlockSpec((1, tn), lambda i, j: (0, j)),
                  pl.BlockSpec((1, tn), lambda i, j: (0, d // tn + j))],
        out_specs=pl.BlockSpec((tm, tn), lambda i, j: (i, j)),
        out_shape=jax.ShapeDtypeStruct((t, d), BF16),
        compiler_params=_cp("arbitrary", "arbitrary"),
        name="merge",
    )(y_ssd, u, w_ssm_out, w_conv_out, b_conv_out.reshape(1, d), proj2, proj2, bg, bg)


def _mixout_kernel(m_ref, w_ref, x_ref, gpost_ref, g1_ref, gpre_ref, sc2_ref, sh2_ref, x1_ref, h2_ref):
    mix = _bdot(m_ref[...], w_ref[...])
    nm = mix * lax.rsqrt(jnp.mean(mix * mix, axis=-1, keepdims=True) + EPS) * gpost_ref[...]
    x1 = x_ref[...] + g1_ref[0] * nm
    x1_ref[...] = x1
    y = x1 * lax.rsqrt(jnp.mean(x1 * x1, axis=-1, keepdims=True) + EPS) * gpre_ref[...]
    h2_ref[...] = (y * (1.0 + sc2_ref[0]) + sh2_ref[0]).astype(h2_ref.dtype)


def _mixout(mix_in, w_mix_out, x2, g_post, g1, g_pre_ffn, sc2, sh2, seq):
    t, d = x2.shape
    tm = _row_tile(seq, 512)
    per_b = seq // tm
    row = lambda i: (0, 0)
    brow = lambda i: (i // per_b, 0, 0)
    return pl.pallas_call(
        _mixout_kernel,
        grid=(t // tm,),
        in_specs=[pl.BlockSpec((tm, d), lambda i: (i, 0)),
                  pl.BlockSpec((d, d), row),
                  pl.BlockSpec((tm, d), lambda i: (i, 0)),
                  pl.BlockSpec((1, d), row),
                  pl.BlockSpec((1, 1, d), brow),
                  pl.BlockSpec((1, d), row),
                  pl.BlockSpec((1, 1, d), brow),
                  pl.BlockSpec((1, 1, d), brow)],
        out_specs=[pl.BlockSpec((tm, d), lambda i: (i, 0)),
                   pl.BlockSpec((tm, d), lambda i: (i, 0))],
        out_shape=[jax.ShapeDtypeStruct((t, d), F32), jax.ShapeDtypeStruct((t, d), BF16)],
        compiler_params=_cp("arbitrary"),
        name="mixout",
    )(mix_in, w_mix_out, x2, g_post, g1, g_pre_ffn, sc2, sh2)


def _ffn_up_kernel(h_ref, wg_ref, wu_ref, o_ref):
    h = h_ref[...]
    o_ref[...] = (_silu(_bdot(h, wg_ref[...])) * _bdot(h, wu_ref[...])).astype(o_ref.dtype)


def _ffn_up(h2, w_gate_up):
    t, d = h2.shape
    d_ff = w_gate_up.shape[1] // 2
    tm = _row_tile(t, 1024)
    tn = _row_tile(d_ff, 512)
    nj = d_ff // tn
    return pl.pallas_call(
        _ffn_up_kernel,
        grid=(t // tm, nj),
        in_specs=[pl.BlockSpec((tm, d), lambda i, j: (i, 0)),
                  pl.BlockSpec((d, tn), lambda i, j: (0, j)),
                  pl.BlockSpec((d, tn), lambda i, j: (0, nj + j))],
        out_specs=pl.BlockSpec((tm, tn), lambda i, j: (i, j)),
        out_shape=jax.ShapeDtypeStruct((t, d_ff), BF16),
        compiler_params=_cp("arbitrary", "arbitrary"),
        name="ffn_up",
    )(h2, w_gate_up, w_gate_up)


def _ffn_down_kernel(a_ref, w_ref, x1_ref, gpost_ref, g2_ref, o_ref):
    k = pl.program_id(1)
    part = _bdot(a_ref[...], w_ref[...])

    @pl.when(k == 0)
    def _():
        o_ref[...] = part

    @pl.when(k > 0)
    def _():
        o_ref[...] += part

    @pl.when(k == pl.num_programs(1) - 1)
    def _():
        f = o_ref[...]
        nf = f * lax.rsqrt(jnp.mean(f * f, axis=-1, keepdims=True) + EPS) * gpost_ref[...]
        o_ref[...] = x1_ref[...] + g2_ref[0] * nf


def _ffn_down(act, w_down, x1, g_post, g2, seq):
    t, d_ff = act.shape
    d = w_down.shape[1]
    tm = _row_tile(seq, 1024)
    tk = _lane_tile(d_ff, 1408)
    per_b = seq // tm
    return pl.pallas_call(
        _ffn_down_kernel,
        grid=(t // tm, d_ff // tk),
        in_specs=[pl.BlockSpec((tm, tk), lambda i, k: (i, k)),
                  pl.BlockSpec((tk, d), lambda i, k: (k, 0)),
                  pl.BlockSpec((tm, d), lambda i, k: (i, 0), pipeline_mode=pl.Buffered(1)),
                  pl.BlockSpec((1, d), lambda i, k: (0, 0)),
                  pl.BlockSpec((1, 1, d), lambda i, k: (i // per_b, 0, 0))],
        out_specs=pl.BlockSpec((tm, d), lambda i, k: (i, 0)),
        out_shape=jax.ShapeDtypeStruct((t, d), F32),
        compiler_params=_cp("arbitrary", "arbitrary"),
        name="ffn_down",
    )(act, w_down, x1, g_post, g2)


def kernel(x, c, w_ada, b_ada, g_pre_mix, g_post_mix, w_in, w_conv_ssm, b_conv_ssm, dt_bias_fwd, dt_bias_bwd, a_log_fwd, a_log_bwd, d_skip, g_ssm_norm, w_ssm_out, b_glu, w_dw, b_dw, ln_g, ln_b, w_conv_out, b_conv_out, b_gate, w_mix_out, g_pre_ffn, g_post_ffn, w_gate_up, w_down):
    bsz, seq, d = x.shape
    depth = w_ada.shape[0]
    n_heads = dt_bias_fwd.shape[1]
    d_ssm = w_ssm_out.shape[1]
    d_xbc = w_conv_ssm.shape[2]
    d_conv = w_dw.shape[2]
    pdim = d_ssm // n_heads
    assert d_xbc == d_ssm + 2 * N_GROUPS * D_STATE and seq % CHUNK == 0 and n_heads % N_GROUPS == 0
    s1, s2, s3 = d_ssm, d_ssm + d_xbc, d_ssm + d_xbc + 2 * n_heads
    glu_off, gate_off = d_ssm, d_ssm + 2 * d_conv

    x2 = x.reshape(bsz * seq, d)
    for l in range(depth):
        w_rest = jnp.concatenate([w_in[l][:, :s1], w_in[l][:, s3:]], axis=1).astype(BF16)
        w_xbc = w_in[l][:, s1:s2].astype(BF16)
        w_dt = w_in[l][:, s2:s3].astype(BF16)
        row = lambda v: v.reshape(1, -1)

        mod = _ada(c, w_ada[l], b_ada[l])
        sh1, sc1, g1, sh2, sc2, g2 = [m.reshape(bsz, 1, d) for m in jnp.split(mod, 6, axis=-1)]

        proj2, dt2 = _inproj(x2, sc1, sh1, row(g_pre_mix[l]), w_rest, w_dt, seq)
        proj3 = proj2.reshape(bsz, seq, -1)
        dt3 = dt2.reshape(bsz, seq, -1)

        xbc3 = _xbc(x2, sc1, sh1, row(g_pre_mix[l]), w_xbc, w_conv_ssm[l], b_conv_ssm[l],
                    seq).reshape(bsz, seq, d_xbc)
        bias_row = row(jnp.concatenate([dt_bias_fwd[l], dt_bias_bwd[l]]))
        alog_row = row(jnp.concatenate([a_log_fwd[l], a_log_bwd[l]]))
        dskip_row = row(jnp.repeat(d_skip[l], pdim))
        y_f = _ssd(False, xbc3, dt3, bias_row, alog_row, (dskip_row,), n_heads, d_ssm)
        y_ssd = _ssd(True, xbc3, dt3, bias_row, alog_row, (y_f, proj3, row(g_ssm_norm[l])), n_heads, d_ssm)

        u = _conf(proj3, glu_off, b_glu[l], w_dw[l], b_dw[l], ln_g[l], ln_b[l])

        mix_in = _merge(y_ssd.reshape(bsz * seq, d_ssm), u.reshape(bsz * seq, d_conv),
                        w_ssm_out[l].astype(BF16), w_conv_out[l].astype(BF16), b_conv_out[l],
                        proj2, gate_off, b_gate[l])
        x1, h2 = _mixout(mix_in, w_mix_out[l].astype(BF16), x2, row(g_post_mix[l]), g1,
                         row(g_pre_ffn[l]), sc2, sh2, seq)
        act = _ffn_up(h2, w_gate_up[l].astype(BF16))
        x2 = _ffn_down(act, w_down[l].astype(BF16), x1, row(g_post_ffn[l]), g2, seq)
    return x2.reshape(bsz, seq, d)
```

```python
import functools

import jax
import jax.numpy as jnp
from jax import lax
from jax.experimental import pallas as pl
from jax.experimental.pallas import tpu as pltpu

F32 = jnp.float32
BF16 = jnp.bfloat16

EPS = 1e-6
N_GROUPS = 8
D_STATE = 128
CHUNK = 128
V7X_VMEM_LIMIT_BYTES = 58 * 1024 * 1024
SUBLANES = 8
LANES = 128
LOG2E = 1.4426950408889634


def _cp(*sem):
    return pltpu.CompilerParams(dimension_semantics=sem, vmem_limit_bytes=V7X_VMEM_LIMIT_BYTES)


def _silu(v):
    return v * jax.nn.sigmoid(v)


def _softplus(v):
    return jnp.maximum(v, 0.0) + jnp.log1p(jnp.exp(-jnp.abs(v)))


def _bdot(a, b):
    return jnp.dot(a, b, preferred_element_type=F32)


def _row_tile(n, want):
    t = min(n, want)
    assert n % t == 0, (n, t)
    return t


def _lane_tile(n, want):
    return max(t for t in range(LANES, min(n, want) + 1, LANES) if n % t == 0)


def _ada_kernel(c_ref, w_ref, b_ref, o_ref):
    ca = _silu(c_ref[...]).astype(BF16)
    o_ref[...] = _bdot(ca, w_ref[...].astype(BF16)) + b_ref[...]


def _ada(c, w, b):
    bsz, d = c.shape
    n = w.shape[1]
    rows = 16
    tn = _row_tile(n, 1024)
    cpad = jnp.zeros((rows, d), F32).at[:bsz].set(c)
    out = pl.pallas_call(
        _ada_kernel,
        grid=(n // tn,),
        in_specs=[pl.BlockSpec((rows, d), lambda j: (0, 0)),
                  pl.BlockSpec((d, tn), lambda j: (0, j)),
                  pl.BlockSpec((1, tn), lambda j: (0, j))],
        out_specs=pl.BlockSpec((rows, tn), lambda j: (0, j)),
        out_shape=jax.ShapeDtypeStruct((rows, n), F32),
        compiler_params=_cp("arbitrary"),
        name="ada",
    )(cpad, w, b.reshape(1, n))
    return out[:bsz]


def _inproj_kernel(x_ref, sc_ref, sh_ref, g_ref, w_ref, wdt_ref, o_ref, odt_ref, h_ref):
    @pl.when(pl.program_id(1) == 0)
    def _():
        x = x_ref[...]
        y = x * lax.rsqrt(jnp.mean(x * x, axis=-1, keepdims=True) + EPS) * g_ref[...]
        h = (y * (1.0 + sc_ref[0]) + sh_ref[0]).astype(BF16)
        h_ref[...] = h
        odt_ref[...] = _bdot(h, wdt_ref[...])

    o_ref[...] = _bdot(h_ref[...], w_ref[...])


def _col_tiles(w, tn):
    d, n = w.shape
    return w.reshape(d, n // tn, tn).transpose(1, 0, 2)


def _inproj(x2, sc, sh, g, w_tiles, w_dt, seq):
    t, d = x2.shape
    ncol, _, tn = w_tiles.shape
    ndt = w_dt.shape[1]
    tm = _row_tile(seq, 1024)
    per_b = seq // tm
    return pl.pallas_call(
        _inproj_kernel,
        grid=(t // tm, ncol),
        in_specs=[pl.BlockSpec((tm, d), lambda i, j: (i, 0)),
                  pl.BlockSpec((1, 1, d), lambda i, j: (i // per_b, 0, 0)),
                  pl.BlockSpec((1, 1, d), lambda i, j: (i // per_b, 0, 0)),
                  pl.BlockSpec((1, d), lambda i, j: (0, 0)),
                  pl.BlockSpec((None, d, tn), lambda i, j: (j, 0, 0)),
                  pl.BlockSpec((d, ndt), lambda i, j: (0, 0))],
        out_specs=[pl.BlockSpec((tm, tn), lambda i, j: (i, j)),
                   pl.BlockSpec((tm, ndt), lambda i, j: (i, 0))],
        out_shape=[jax.ShapeDtypeStruct((t, ncol * tn), F32),
                   jax.ShapeDtypeStruct((t, ndt), F32)],
        scratch_shapes=[pltpu.VMEM((tm, d), BF16)],
        compiler_params=_cp("arbitrary", "arbitrary"),
        name="inproj",
    )(x2, sc, sh, g, w_tiles, w_dt)


def _fill_ext(ext_ref, prev, main, nxt, halo, ts):
    s = pl.program_id(1)
    last = pl.num_programs(1) - 1
    ext_ref[0:halo, :] = jnp.where(s > 0, prev, 0.0)
    ext_ref[halo:halo + ts, :] = main
    ext_ref[halo + ts:halo + ts + halo, :] = jnp.where(s < last, nxt, 0.0)


def _dwconv_rows(tap_rows, w_ref, b_ref, rb, lanes, taps):
    acc = jnp.broadcast_to(b_ref[:, lanes], (rb, lanes.stop - lanes.start))
    for k in range(taps):
        acc = acc + tap_rows(k) * w_ref[k:k + 1, lanes]
    return acc


def _halo_specs(ts, halo, seq, width, col_block):
    per = ts // halo
    nh = seq // halo

    def prev_map(b, s, *c):
        return (b, jnp.maximum(s * per - 1, 0), col_block(*c))

    def main_map(b, s, *c):
        return (b, s, col_block(*c))

    def next_map(b, s, *c):
        return (b, jnp.minimum((s + 1) * per, nh - 1), col_block(*c))

    return [pl.BlockSpec((1, halo, width), prev_map),
            pl.BlockSpec((1, ts, width), main_map),
            pl.BlockSpec((1, halo, width), next_map)]


def _xbc_kernel(taps, halo, tm, ncol, per_b, strip, rb, kc,
                xp_ref, xm_ref, xn_ref, sc_ref, sh_ref, g_ref, w_ref, cw_ref, cb_ref,
                o_ref, h_ref, raw_a, raw_b, shift_ref):
    s = pl.program_id(0)
    ntiles = pl.num_programs(0) - 1
    cur = jnp.minimum(s, ntiles - 1)
    tn = o_ref.shape[-1]
    first = halo - (taps - 1) // 2
    shifts = sorted({(first + k) % SUBLANES for k in range(taps)} - {0})
    nrows = shift_ref.shape[1]

    @pl.when(s == 0)
    def _():
        raw_b[...] = jnp.zeros_like(raw_b)

    @pl.when((s < ntiles) & (cur % ncol == 0))
    def _():
        def modulated(x):
            y = x * lax.rsqrt(jnp.mean(x * x, axis=-1, keepdims=True) + EPS) * g_ref[...]
            return y * (1.0 + sc_ref[0]) + sh_ref[0]

        h_ref[...] = jnp.concatenate(
            [modulated(xp_ref[...]), modulated(xm_ref[...]), modulated(xn_ref[...])], axis=0).astype(BF16)

    def step(raw_w, raw_r):
        pos = (jnp.maximum(s - 1, 0) // ncol) % per_b
        raw_r[0:halo, :] = jnp.where(pos > 0, raw_r[0:halo, :], 0.0)
        raw_r[halo + tm:, :] = jnp.where(pos < per_b - 1, raw_r[halo + tm:, :], 0.0)
        pieces = []
        for c0 in range(0, tn, strip):
            lanes = slice(c0, c0 + strip)

            def copy_shifts(lanes=lanes):
                for n, q in enumerate(shifts):
                    shift_ref[n] = raw_r[q:q + nrows, lanes]

            def tap_rows(r0, k, lanes=lanes):
                m, q = divmod(first + k, SUBLANES)
                rows = slice(r0 + SUBLANES * m, r0 + SUBLANES * m + rb)
                return raw_r[rows, lanes] if q == 0 else shift_ref[shifts.index(q), rows, :]

            def conv_rows(r0, lanes=lanes, tap_rows=tap_rows):
                acc = _dwconv_rows(functools.partial(tap_rows, r0), cw_ref, cb_ref, rb, lanes, taps)
                o_ref[r0:r0 + rb, lanes] = _silu(acc)

            pieces.append(copy_shifts)
            pieces += [functools.partial(conv_rows, r0) for r0 in range(0, tm, rb)]
        nk = h_ref.shape[1] // kc
        per_chunk = -(-len(pieces) // nk)
        for c in range(nk):
            part = _bdot(h_ref[:, c * kc:(c + 1) * kc], w_ref[c * kc:(c + 1) * kc, :])
            if c == 0:
                raw_w[...] = part
            else:
                raw_w[...] += part
            for piece in pieces[c * per_chunk:(c + 1) * per_chunk]:
                piece()

    @pl.when(s % 2 == 0)
    def _():
        step(raw_a, raw_b)

    @pl.when(s % 2 == 1)
    def _():
        step(raw_b, raw_a)


def _xbc(x2, sc, sh, g, w_xbc, cw, cb, seq):
    t, d = x2.shape
    n = cw.shape[1]
    taps = cw.shape[0]
    halo = SUBLANES
    tm = _row_tile(seq, 1024)
    tn = _row_tile(n, 512)
    strip = _row_tile(tn, 256)
    rb = 32
    per_b = seq // tm
    ncol = n // tn
    w_tiles = _col_tiles(w_xbc, tn)
    ntiles = (t // tm) * ncol
    per = tm // halo
    nh = t // halo

    def row_tile(s):
        return jnp.minimum(s, ntiles - 1) // ncol

    def col_tile(s):
        return jnp.minimum(s, ntiles - 1) % ncol

    def out_map(s):
        prev = jnp.maximum(s - 1, 0)
        return (prev // ncol, prev % ncol)

    return pl.pallas_call(
        functools.partial(_xbc_kernel, taps, halo, tm, ncol, per_b, strip, rb, _row_tile(d, 256)),
        grid=(ntiles + 1,),
        in_specs=[pl.BlockSpec((halo, d), lambda s: (jnp.maximum(row_tile(s) * per - 1, 0), 0)),
                  pl.BlockSpec((tm, d), lambda s: (row_tile(s), 0)),
                  pl.BlockSpec((halo, d), lambda s: (jnp.minimum((row_tile(s) + 1) * per, nh - 1), 0)),
                  pl.BlockSpec((1, 1, d), lambda s: (row_tile(s) // per_b, 0, 0)),
                  pl.BlockSpec((1, 1, d), lambda s: (row_tile(s) // per_b, 0, 0)),
                  pl.BlockSpec((1, d), lambda s: (0, 0)),
                  pl.BlockSpec((None, d, tn), lambda s: (col_tile(s), 0, 0)),
                  pl.BlockSpec((taps, tn), lambda s: (0, out_map(s)[1])),
                  pl.BlockSpec((1, tn), lambda s: (0, out_map(s)[1]))],
        out_specs=pl.BlockSpec((tm, tn), out_map),
        out_shape=jax.ShapeDtypeStruct((t, n), F32),
        scratch_shapes=[pltpu.VMEM((tm + 2 * halo, d), BF16),
                        pltpu.VMEM((tm + 2 * halo, tn), F32),
                        pltpu.VMEM((tm + 2 * halo, tn), F32),
                        pltpu.VMEM((min(taps, SUBLANES) - 1, tm + 2 * halo - SUBLANES, strip), F32)],
        compiler_params=_cp("arbitrary"),
        name="xbc",
    )(x2, x2, x2, sc, sh, g, w_tiles, cw, cb.reshape(1, n))


def _ssd_kernel(reverse, n_heads, hpg, pdim, *refs):
    if reverse:
        (xs_ref, b_ref, c_ref, dt_ref, bias_ref, alog_ref, yf_ref, z_ref, gn_ref,
         o_ref, h_ref, dt_t, cs_t, wend_t, cdec_t) = refs
    else:
        (xs_ref, b_ref, c_ref, dt_ref, bias_ref, alog_ref, dskip_ref,
         o_ref, h_ref, dt_t, cs_t, wend_t, cdec_t) = refs
    ln = CHUNK
    w = hpg * pdim
    hd2 = 2 * n_heads
    nt = (((1,), (1,)), ((), ()))

    @pl.when(pl.program_id(1) == 0)
    def _():
        h_ref[...] = jnp.zeros_like(h_ref)

    ri = lax.broadcasted_iota(jnp.int32, (ln, ln), 0)
    ci = lax.broadcasted_iota(jnp.int32, (ln, ln), 1)
    mask = (ci >= ri) if reverse else (ci <= ri)
    cum = mask.astype(BF16)
    edge = 0 if reverse else ln - 1

    dt_all = _softplus(dt_ref[0] + bias_ref[...])
    a_all = dt_all * (-jnp.exp(alog_ref[...]) * LOG2E)
    a_hi = a_all.astype(BF16)
    r1 = a_all - a_hi.astype(F32)
    a_mid = r1.astype(BF16)
    a_lo = (r1 - a_mid.astype(F32)).astype(BF16)
    cs_all = _bdot(cum, a_hi) + _bdot(cum, a_mid) + _bdot(cum, a_lo)
    cs_edge = cs_all[edge:edge + 1, :]
    dt_t[...] = dt_all.T
    cs_t[...] = cs_all.T
    wend_t[...] = (dt_all * jnp.exp2(cs_edge - cs_all)).T
    cdec_t[...] = jnp.broadcast_to(jnp.exp2(cs_edge), (ln, hd2)).T

    dirbase = n_heads if reverse else 0
    for g in range(N_GROUPS):
        lanes = slice(g * w, (g + 1) * w)
        xs = xs_ref[0, :, lanes]
        x_t = xs.T
        bgb = b_ref[0, :, g * D_STATE:(g + 1) * D_STATE].astype(BF16)
        cgb = c_ref[0, :, g * D_STATE:(g + 1) * D_STATE].astype(BF16)
        cb = lax.dot_general(cgb, bgb, nt, preferred_element_type=F32)
        h_in = h_ref[g]
        yoff = lax.dot_general(cgb, h_in.astype(BF16), nt, preferred_element_type=F32)
        y_parts, xw_parts, cdec_parts = [], [], []
        for jp in range(hpg // 2):
            ms, xdts, cols = [], [], []
            for j in (2 * jp, 2 * jp + 1):
                c = dirbase + g * hpg + j
                rows = slice(j * pdim, (j + 1) * pdim)
                xdts.append((x_t[rows] * dt_t[c:c + 1, :]).astype(BF16))
                xw_parts.append((x_t[rows] * wend_t[c:c + 1, :]).astype(BF16))
                col = jnp.broadcast_to(cs_all[:, c:c + 1], (ln, ln))
                seg = col - cs_t[c:c + 1, :]
                ms.append((cb * jnp.where(mask, jnp.exp2(seg), 0.0)).astype(BF16))
                cols.append(col)
                cdec_parts.append(jnp.broadcast_to(cdec_t[c:c + 1, :], (pdim, D_STATE)))
            zero = jnp.zeros((pdim, ln), BF16)
            xdt_bd = jnp.concatenate([jnp.concatenate([xdts[0], zero], axis=1),
                                      jnp.concatenate([zero, xdts[1]], axis=1)], axis=0)
            yd = lax.dot_general(jnp.concatenate(ms, axis=1), xdt_bd, nt,
                                 preferred_element_type=F32)
            din = jnp.exp2(jnp.where(ci < pdim, cols[0], cols[1]))
            y_parts.append(yd + yoff[:, 2 * jp * pdim:(2 * jp + 2) * pdim] * din)
        y = jnp.concatenate(y_parts, axis=1)
        st = _bdot(jnp.concatenate(xw_parts, axis=0), bgb)
        h_ref[g] = h_in * jnp.concatenate(cdec_parts, axis=0) + st
        if reverse:
            y = (y + yf_ref[0, :, lanes]) * _silu(z_ref[0, :, lanes])
            y = y * lax.rsqrt(jnp.mean(y * y, axis=-1, keepdims=True) + EPS) * gn_ref[:, lanes]
            o_ref[0, :, lanes] = y.astype(o_ref.dtype)
        else:
            o_ref[0, :, lanes] = y + dskip_ref[:, lanes] * xs


def _ssd(reverse, xbc3, dt3, bias_row, alog_row, extra, n_heads, d_ssm):
    bsz, seq, _ = xbc3.shape
    hd2 = 2 * n_heads
    hpg = n_heads // N_GROUPS
    pdim = d_ssm // n_heads
    w = hpg * pdim
    nc = seq // CHUNK
    ln = CHUNK
    gn = N_GROUPS * D_STATE
    assert CHUNK == D_STATE and d_ssm % gn == 0 and 2 * pdim == CHUNK and hpg % 2 == 0

    def chunk(b, z):
        return (b, nc - 1 - z if reverse else z)

    in_specs = [pl.BlockSpec((1, ln, d_ssm), lambda b, z: (*chunk(b, z), 0)),
                pl.BlockSpec((1, ln, gn), lambda b, z: (*chunk(b, z), d_ssm // gn)),
                pl.BlockSpec((1, ln, gn), lambda b, z: (*chunk(b, z), d_ssm // gn + 1)),
                pl.BlockSpec((1, ln, hd2), lambda b, z: (*chunk(b, z), 0)),
                pl.BlockSpec((1, hd2), lambda b, z: (0, 0)),
                pl.BlockSpec((1, hd2), lambda b, z: (0, 0))]
    args = [xbc3, xbc3, xbc3, dt3, bias_row, alog_row]
    if reverse:
        y_f, proj3, gnorm = extra
        in_specs += [pl.BlockSpec((1, ln, d_ssm), lambda b, z: (*chunk(b, z), 0)),
                     pl.BlockSpec((1, ln, d_ssm), lambda b, z: (*chunk(b, z), 0)),
                     pl.BlockSpec((1, d_ssm), lambda b, z: (0, 0))]
        args += [y_f, proj3, gnorm]
        out_dtype = BF16
    else:
        (dskip,) = extra
        in_specs += [pl.BlockSpec((1, d_ssm), lambda b, z: (0, 0))]
        args += [dskip]
        out_dtype = F32
    table = pltpu.VMEM((hd2, ln), F32)
    return pl.pallas_call(
        functools.partial(_ssd_kernel, reverse, n_heads, hpg, pdim),
        grid=(bsz, nc),
        in_specs=in_specs,
        out_specs=pl.BlockSpec((1, ln, d_ssm), lambda b, z: (*chunk(b, z), 0)),
        out_shape=jax.ShapeDtypeStruct((bsz, seq, d_ssm), out_dtype),
        scratch_shapes=[pltpu.VMEM((N_GROUPS, w, D_STATE), F32), table, table, table, table],
        compiler_params=_cp("arbitrary", "arbitrary"),
        name="ssd_bwd" if reverse else "ssd_fwd",
    )(*args)


def _conf_kernel(taps, halo, ts, rb, lc, pa_ref, ma_ref, na_ref, pb_ref, mb_ref, nb_ref,
                 bga_ref, bgb_ref, w_ref, bdw_ref, lng_ref, lnb_ref, o_ref, ext_ref, shift_ref, conv_ref):
    d = o_ref.shape[-1]

    def glu(a, b):
        return (a + bga_ref[...]) * jax.nn.sigmoid(b + bgb_ref[...])

    _fill_ext(ext_ref, glu(pa_ref[0], pb_ref[0]), glu(ma_ref[0], mb_ref[0]),
              glu(na_ref[0], nb_ref[0]), halo, ts)
    nrows = shift_ref.shape[1]
    for q in range(SUBLANES):
        shift_ref[q] = ext_ref[q:q + nrows, :]
    first = halo - (taps - 1) // 2

    for c0 in range(0, d, lc):
        lanes = slice(c0, c0 + lc)
        nblk = ts // SUBLANES
        acc = [jnp.broadcast_to(bdw_ref[:, lanes], (SUBLANES, lc))] * nblk
        for q in range(SUBLANES):
            ms = [(first + k) // SUBLANES for k in range(taps) if (first + k) % SUBLANES == q]
            w_b = {m: jnp.broadcast_to(w_ref[SUBLANES * m + q - first:SUBLANES * m + q - first + 1, lanes],
                                       (SUBLANES, lc)) for m in ms}
            for i in range(nblk + max(ms)):
                blk = shift_ref[q, SUBLANES * i:SUBLANES * (i + 1), lanes]
                for m in ms:
                    if 0 <= i - m < nblk:
                        acc[i - m] = acc[i - m] + blk * w_b[m]
        for i in range(nblk):
            conv_ref[SUBLANES * i:SUBLANES * (i + 1), lanes] = acc[i]

    u = conv_ref[...]
    uc = u - jnp.mean(u, axis=-1, keepdims=True)
    y = uc * lax.rsqrt(jnp.mean(uc * uc, axis=-1, keepdims=True) + EPS)
    y = y * lng_ref[...] + lnb_ref[...]
    o_ref[0] = _silu(y).astype(o_ref.dtype)


def _conf(proj3, col_off, b_glu, w_dw, b_dw, ln_g, ln_b):
    bsz, seq, _ = proj3.shape
    taps, d = w_dw.shape
    halo = 16
    ts = _row_tile(seq, 256)
    rb = 64
    lc = 128
    ablk = col_off // d
    specs_a = _halo_specs(ts, halo, seq, d, lambda: ablk)
    specs_b = _halo_specs(ts, halo, seq, d, lambda: ablk + 1)
    row = lambda b_, s: (0, 0)
    return pl.pallas_call(
        functools.partial(_conf_kernel, taps, halo, ts, rb, lc),
        grid=(bsz, seq // ts),
        in_specs=specs_a + specs_b + [pl.BlockSpec((1, d), row), pl.BlockSpec((1, d), row),
                                      pl.BlockSpec((taps, d), row), pl.BlockSpec((1, d), row),
                                      pl.BlockSpec((1, d), row), pl.BlockSpec((1, d), row)],
        out_specs=pl.BlockSpec((1, ts, d), lambda b_, s: (b_, s, 0)),
        out_shape=jax.ShapeDtypeStruct((bsz, seq, d), BF16),
        scratch_shapes=[pltpu.VMEM((ts + 2 * halo, d), F32),
                        pltpu.VMEM((SUBLANES, ts + 2 * halo - SUBLANES, d), F32),
                        pltpu.VMEM((ts, d), F32)],
        compiler_params=_cp("arbitrary", "arbitrary"),
        name="conf",
    )(proj3, proj3, proj3, proj3, proj3, proj3,
      b_glu[:d].reshape(1, d), b_glu[d:].reshape(1, d), w_dw, b_dw.reshape(1, d),
      ln_g.reshape(1, d), ln_b.reshape(1, d))


def _merge_kernel(ya_ref, u_ref, wa_ref, wb_ref, bb_ref, ga_ref, gb_ref, bga_ref, bgb_ref, o_ref):
    y_a = _bdot(ya_ref[...], wa_ref[...])
    y_b = _bdot(u_ref[...], wb_ref[...]) + bb_ref[...]
    g_a = jax.nn.sigmoid(ga_ref[...] + bga_ref[...])
    g_b = jax.nn.sigmoid(gb_ref[...] + bgb_ref[...])
    o_ref[...] = (g_a * y_a + g_b * y_b).astype(o_ref.dtype)


def _merge(y_ssd, u, w_ssm_out, w_conv_out, b_conv_out, proj2, gate_off, b_gate):
    t, d_ssm = y_ssd.shape
    d_conv = u.shape[1]
    d = w_ssm_out.shape[1]
    tm = _row_tile(t, 1024)
    tn = _row_tile(d, 512)
    ga = gate_off // tn
    gb = (gate_off + d) // tn
    bg = b_gate.reshape(1, 2 * d)
    return pl.pallas_call(
        _merge_kernel,
        grid=(t // tm, d // tn),
        in_specs=[pl.BlockSpec((tm, d_ssm), lambda i, j: (i, 0)),
                  pl.BlockSpec((tm, d_conv), lambda i, j: (i, 0)),
                  pl.BlockSpec((None, d_ssm, tn), lambda i, j: (j, 0, 0)),
                  pl.BlockSpec((None, d_conv, tn), lambda i, j: (j, 0, 0)),
                  pl.BlockSpec((1, tn), lambda i, j: (0, j)),
                  pl.BlockSpec((tm, tn), lambda i, j: (i, ga + j)),
                  pl.BlockSpec((tm, tn), lambda i, j: (i, gb + j)),
                  pl.BlockSpec((1, tn), lambda i, j: (0, j)),
                  pl.BlockSpec((1, tn), lambda i, j: (0, d // tn + j))],
        out_specs=pl.BlockSpec((tm, tn), lambda i, j: (i, j)),
        out_shape=jax.ShapeDtypeStruct((t, d), BF16),
        compiler_params=_cp("arbitrary", "arbitrary"),
        name="merge",
    )(y_ssd, u, _col_tiles(w_ssm_out, tn), _col_tiles(w_conv_out, tn), b_conv_out.reshape(1, d),
      proj2, proj2, bg, bg)


def _mixout_kernel(m_ref, w_ref, x_ref, gpost_ref, g1_ref, gpre_ref, sc2_ref, sh2_ref, x1_ref, h2_ref):
    mix = _bdot(m_ref[...], w_ref[...])
    nm = mix * lax.rsqrt(jnp.mean(mix * mix, axis=-1, keepdims=True) + EPS) * gpost_ref[...]
    x1 = x_ref[...] + g1_ref[0] * nm
    x1_ref[...] = x1
    y = x1 * lax.rsqrt(jnp.mean(x1 * x1, axis=-1, keepdims=True) + EPS) * gpre_ref[...]
    h2_ref[...] = (y * (1.0 + sc2_ref[0]) + sh2_ref[0]).astype(h2_ref.dtype)


def _mixout(mix_in, w_mix_out, x2, g_post, g1, g_pre_ffn, sc2, sh2, seq):
    t, d = x2.shape
    tm = _row_tile(seq, 512)
    per_b = seq // tm
    row = lambda i: (0, 0)
    brow = lambda i: (i // per_b, 0, 0)
    return pl.pallas_call(
        _mixout_kernel,
        grid=(t // tm,),
        in_specs=[pl.BlockSpec((tm, d), lambda i: (i, 0)),
                  pl.BlockSpec((d, d), row),
                  pl.BlockSpec((tm, d), lambda i: (i, 0)),
                  pl.BlockSpec((1, d), row),
                  pl.BlockSpec((1, 1, d), brow),
                  pl.BlockSpec((1, d), row),
                  pl.BlockSpec((1, 1, d), brow),
                  pl.BlockSpec((1, 1, d), brow)],
        out_specs=[pl.BlockSpec((tm, d), lambda i: (i, 0)),
                   pl.BlockSpec((tm, d), lambda i: (i, 0))],
        out_shape=[jax.ShapeDtypeStruct((t, d), F32), jax.ShapeDtypeStruct((t, d), BF16)],
        compiler_params=_cp("arbitrary"),
        name="mixout",
    )(mix_in, w_mix_out, x2, g_post, g1, g_pre_ffn, sc2, sh2)


def _ffn_up_kernel(h_ref, wg_ref, wu_ref, o_ref):
    h = h_ref[...]
    o_ref[...] = (_silu(_bdot(h, wg_ref[...])) * _bdot(h, wu_ref[...])).astype(o_ref.dtype)


def _ffn_up(h2, w_gate_up):
    t, d = h2.shape
    d_ff = w_gate_up.shape[1] // 2
    tm = _row_tile(t, 1024)
    tn = _row_tile(d_ff, 512)
    nj = d_ff // tn
    w_tiles = _col_tiles(w_gate_up, tn)
    return pl.pallas_call(
        _ffn_up_kernel,
        grid=(t // tm, nj),
        in_specs=[pl.BlockSpec((tm, d), lambda i, j: (i, 0)),
                  pl.BlockSpec((None, d, tn), lambda i, j: (j, 0, 0)),
                  pl.BlockSpec((None, d, tn), lambda i, j: (nj + j, 0, 0))],
        out_specs=pl.BlockSpec((tm, tn), lambda i, j: (i, j)),
        out_shape=jax.ShapeDtypeStruct((t, d_ff), BF16),
        compiler_params=_cp("arbitrary", "arbitrary"),
        name="ffn_up",
    )(h2, w_tiles, w_tiles)


def _ffn_down_kernel(a_ref, w_ref, x1_ref, gpost_ref, g2_ref, o_ref):
    f = _bdot(a_ref[...], w_ref[...])
    nf = f * lax.rsqrt(jnp.mean(f * f, axis=-1, keepdims=True) + EPS) * gpost_ref[...]
    o_ref[...] = x1_ref[...] + g2_ref[0] * nf


def _ffn_down(act, w_down, x1, g_post, g2, seq):
    t, d_ff = act.shape
    d = w_down.shape[1]
    tm = _row_tile(seq, 512)
    per_b = seq // tm
    return pl.pallas_call(
        _ffn_down_kernel,
        grid=(t // tm,),
        in_specs=[pl.BlockSpec((tm, d_ff), lambda i: (i, 0)),
                  pl.BlockSpec((d_ff, d), lambda i: (0, 0), pipeline_mode=pl.Buffered(1)),
                  pl.BlockSpec((tm, d), lambda i: (i, 0)),
                  pl.BlockSpec((1, d), lambda i: (0, 0)),
                  pl.BlockSpec((1, 1, d), lambda i: (i // per_b, 0, 0))],
        out_specs=pl.BlockSpec((tm, d), lambda i: (i, 0)),
        out_shape=jax.ShapeDtypeStruct((t, d), F32),
        compiler_params=_cp("arbitrary"),
        name="ffn_down",
    )(act, w_down, x1, g_post, g2)


def kernel(x, c, w_ada, b_ada, g_pre_mix, g_post_mix, w_in, w_conv_ssm, b_conv_ssm, dt_bias_fwd, dt_bias_bwd, a_log_fwd, a_log_bwd, d_skip, g_ssm_norm, w_ssm_out, b_glu, w_dw, b_dw, ln_g, ln_b, w_conv_out, b_conv_out, b_gate, w_mix_out, g_pre_ffn, g_post_ffn, w_gate_up, w_down):
    bsz, seq, d = x.shape
    depth = w_ada.shape[0]
    n_heads = dt_bias_fwd.shape[1]
    d_ssm = w_ssm_out.shape[1]
    d_xbc = w_conv_ssm.shape[2]
    d_conv = w_dw.shape[2]
    pdim = d_ssm // n_heads
    assert d_xbc == d_ssm + 2 * N_GROUPS * D_STATE and seq % CHUNK == 0 and n_heads % N_GROUPS == 0
    s1, s2, s3 = d_ssm, d_ssm + d_xbc, d_ssm + d_xbc + 2 * n_heads
    glu_off, gate_off = d_ssm, d_ssm + 2 * d_conv

    x2 = x.reshape(bsz * seq, d)
    for l in range(depth):
        w_zgg = jnp.concatenate([w_in[l][:, :s1], w_in[l][:, s3:]], axis=1).astype(BF16)
        w_xbc = w_in[l][:, s1:s2].astype(BF16)
        w_dt = w_in[l][:, s2:s3].astype(BF16)
        row = lambda v: v.reshape(1, -1)

        mod = _ada(c, w_ada[l], b_ada[l])
        sh1, sc1, g1, sh2, sc2, g2 = [m.reshape(bsz, 1, d) for m in jnp.split(mod, 6, axis=-1)]

        proj2, dt2 = _inproj(x2, sc1, sh1, row(g_pre_mix[l]),
                             _col_tiles(w_zgg, _row_tile(s1, 1024)), w_dt, seq)
        proj3 = proj2.reshape(bsz, seq, -1)
        dt3 = dt2.reshape(bsz, seq, -1)

        xbc3 = _xbc(x2, sc1, sh1, row(g_pre_mix[l]), w_xbc, w_conv_ssm[l], b_conv_ssm[l],
                    seq).reshape(bsz, seq, d_xbc)
        bias_row = row(jnp.concatenate([dt_bias_fwd[l], dt_bias_bwd[l]]))
        alog_row = row(jnp.concatenate([a_log_fwd[l], a_log_bwd[l]]))
        dskip_row = row(jnp.repeat(d_skip[l], pdim))
        y_f = _ssd(False, xbc3, dt3, bias_row, alog_row, (dskip_row,), n_heads, d_ssm)
        y_ssd = _ssd(True, xbc3, dt3, bias_row, alog_row, (y_f, proj3, row(g_ssm_norm[l])), n_heads, d_ssm)

        u = _conf(proj3, glu_off, b_glu[l], w_dw[l], b_dw[l], ln_g[l], ln_b[l])

        mix_in = _merge(y_ssd.reshape(bsz * seq, d_ssm), u.reshape(bsz * seq, d_conv),
                        w_ssm_out[l].astype(BF16), w_conv_out[l].astype(BF16), b_conv_out[l],
                        proj2, gate_off, b_gate[l])
        x1, h2 = _mixout(mix_in, w_mix_out[l].astype(BF16), x2, row(g_post_mix[l]), g1,
                         row(g_pre_ffn[l]), sc2, sh2, seq)
        act = _ffn_up(h2, w_gate_up[l].astype(BF16))
        x2 = _ffn_down(act, w_down[l].astype(BF16), x1, row(g_post_ffn[l]), g2, seq)
    return x2.reshape(bsz, seq, d)
```

```python
import functools

import jax
import jax.numpy as jnp
from jax import lax
from jax.experimental import pallas as pl
from jax.experimental.pallas import tpu as pltpu

F32 = jnp.float32
BF16 = jnp.bfloat16

EPS = 1e-6
N_GROUPS = 8
D_STATE = 128
CHUNK = 128
V7X_VMEM_LIMIT_BYTES = 58 * 1024 * 1024
SUBLANES = 8
LANES = 128
LOG2E = 1.4426950408889634


def _cp(*sem):
    return pltpu.CompilerParams(dimension_semantics=sem, vmem_limit_bytes=V7X_VMEM_LIMIT_BYTES)


def _silu(v):
    return v * jax.nn.sigmoid(v)


def _softplus(v):
    return jnp.maximum(v, 0.0) + jnp.log1p(jnp.exp(-jnp.abs(v)))


def _bdot(a, b):
    return jnp.dot(a, b, preferred_element_type=F32)


def _row_tile(n, want):
    t = min(n, want)
    assert n % t == 0, (n, t)
    return t


def _lane_tile(n, want):
    return max(t for t in range(LANES, min(n, want) + 1, LANES) if n % t == 0)


def _ada_kernel(c_ref, w_ref, b_ref, o_ref):
    ca = _silu(c_ref[...]).astype(BF16)
    o_ref[...] = _bdot(ca, w_ref[...].astype(BF16)) + b_ref[...]


def _ada(c, w, b):
    bsz, d = c.shape
    n = w.shape[1]
    rows = 16
    tn = _row_tile(n, 1024)
    cpad = jnp.zeros((rows, d), F32).at[:bsz].set(c)
    out = pl.pallas_call(
        _ada_kernel,
        grid=(n // tn,),
        in_specs=[pl.BlockSpec((rows, d), lambda j: (0, 0)),
                  pl.BlockSpec((d, tn), lambda j: (0, j)),
                  pl.BlockSpec((1, tn), lambda j: (0, j))],
        out_specs=pl.BlockSpec((rows, tn), lambda j: (0, j)),
        out_shape=jax.ShapeDtypeStruct((rows, n), F32),
        compiler_params=_cp("arbitrary"),
        name="ada",
    )(cpad, w, b.reshape(1, n))
    return out[:bsz]


def _inproj_kernel(x_ref, sc_ref, sh_ref, g_ref, w_ref, wdt_ref, o_ref, odt_ref, h_ref):
    @pl.when(pl.program_id(1) == 0)
    def _():
        x = x_ref[...]
        y = x * lax.rsqrt(jnp.mean(x * x, axis=-1, keepdims=True) + EPS) * g_ref[...]
        h = (y * (1.0 + sc_ref[0]) + sh_ref[0]).astype(BF16)
        h_ref[...] = h
        odt_ref[...] = _bdot(h, wdt_ref[...])

    o_ref[...] = _bdot(h_ref[...], w_ref[...])


def _inproj(x2, sc, sh, g, w_main, w_all, dt_off, ndt, seq):
    t, d = x2.shape
    n = w_main.shape[1]
    tm = _row_tile(seq, 1024)
    tn = _row_tile(n, 1024)
    ncol = n // tn
    assert dt_off % ndt == 0
    per_b = seq // tm
    return pl.pallas_call(
        _inproj_kernel,
        grid=(t // tm, ncol),
        in_specs=[pl.BlockSpec((tm, d), lambda i, j: (i, 0)),
                  pl.BlockSpec((1, 1, d), lambda i, j: (i // per_b, 0, 0)),
                  pl.BlockSpec((1, 1, d), lambda i, j: (i // per_b, 0, 0)),
                  pl.BlockSpec((1, d), lambda i, j: (0, 0)),
                  pl.BlockSpec((d, tn), lambda i, j: (0, j)),
                  pl.BlockSpec((d, ndt), lambda i, j: (0, dt_off // ndt))],
        out_specs=[pl.BlockSpec((tm, tn), lambda i, j: (i, j)),
                   pl.BlockSpec((tm, ndt), lambda i, j: (i, 0))],
        out_shape=[jax.ShapeDtypeStruct((t, ncol * tn), F32),
                   jax.ShapeDtypeStruct((t, ndt), F32)],
        scratch_shapes=[pltpu.VMEM((tm, d), BF16)],
        compiler_params=_cp("arbitrary", "arbitrary"),
        name="inproj",
    )(x2, sc, sh, g, w_main, w_all)


def _fill_ext(ext_ref, prev, main, nxt, halo, ts):
    s = pl.program_id(1)
    last = pl.num_programs(1) - 1
    ext_ref[0:halo, :] = jnp.where(s > 0, prev, 0.0)
    ext_ref[halo:halo + ts, :] = main
    ext_ref[halo + ts:halo + ts + halo, :] = jnp.where(s < last, nxt, 0.0)


def _dwconv_rows(tap_rows, w_ref, b_ref, rb, lanes, taps):
    acc = jnp.broadcast_to(b_ref[:, lanes], (rb, lanes.stop - lanes.start))
    for k in range(taps):
        acc = acc + tap_rows(k) * w_ref[k:k + 1, lanes]
    return acc


def _halo_specs(ts, halo, seq, width, col_block):
    per = ts // halo
    nh = seq // halo

    def prev_map(b, s, *c):
        return (b, jnp.maximum(s * per - 1, 0), col_block(*c))

    def main_map(b, s, *c):
        return (b, s, col_block(*c))

    def next_map(b, s, *c):
        return (b, jnp.minimum((s + 1) * per, nh - 1), col_block(*c))

    return [pl.BlockSpec((1, halo, width), prev_map),
            pl.BlockSpec((1, ts, width), main_map),
            pl.BlockSpec((1, halo, width), next_map)]


def _xbc_kernel(taps, halo, tm, ncol, per_b, strip, rb, kc,
                xp_ref, xm_ref, xn_ref, sc_ref, sh_ref, g_ref, w_ref, cw_ref, cb_ref,
                o_ref, h_ref, raw_a, raw_b, shift_ref):
    s = pl.program_id(0)
    ntiles = pl.num_programs(0) - 1
    cur = jnp.minimum(s, ntiles - 1)
    tn = o_ref.shape[-1]
    first = halo - (taps - 1) // 2
    shifts = sorted({(first + k) % SUBLANES for k in range(taps)} - {0})
    nrows = shift_ref.shape[1]

    @pl.when(s == 0)
    def _():
        raw_b[...] = jnp.zeros_like(raw_b)

    @pl.when((s < ntiles) & (cur % ncol == 0))
    def _():
        def modulated(x):
            y = x * lax.rsqrt(jnp.mean(x * x, axis=-1, keepdims=True) + EPS) * g_ref[...]
            return y * (1.0 + sc_ref[0]) + sh_ref[0]

        h_ref[...] = jnp.concatenate(
            [modulated(xp_ref[...]), modulated(xm_ref[...]), modulated(xn_ref[...])], axis=0).astype(BF16)

    def step(raw_w, raw_r):
        pos = (jnp.maximum(s - 1, 0) // ncol) % per_b
        raw_r[0:halo, :] = jnp.where(pos > 0, raw_r[0:halo, :], 0.0)
        raw_r[halo + tm:, :] = jnp.where(pos < per_b - 1, raw_r[halo + tm:, :], 0.0)
        pieces = []
        for c0 in range(0, tn, strip):
            lanes = slice(c0, c0 + strip)

            def copy_shifts(lanes=lanes):
                for n, q in enumerate(shifts):
                    shift_ref[n] = raw_r[q:q + nrows, lanes]

            def tap_rows(r0, k, lanes=lanes):
                m, q = divmod(first + k, SUBLANES)
                rows = slice(r0 + SUBLANES * m, r0 + SUBLANES * m + rb)
                return raw_r[rows, lanes] if q == 0 else shift_ref[shifts.index(q), rows, :]

            def conv_rows(r0, lanes=lanes, tap_rows=tap_rows):
                acc = _dwconv_rows(functools.partial(tap_rows, r0), cw_ref, cb_ref, rb, lanes, taps)
                o_ref[r0:r0 + rb, lanes] = _silu(acc)

            pieces.append(copy_shifts)
            pieces += [functools.partial(conv_rows, r0) for r0 in range(0, tm, rb)]
        nk = h_ref.shape[1] // kc
        per_chunk = -(-len(pieces) // nk)
        for c in range(nk):
            part = _bdot(h_ref[:, c * kc:(c + 1) * kc], w_ref[c * kc:(c + 1) * kc, :])
            if c == 0:
                raw_w[...] = part
            else:
                raw_w[...] += part
            for piece in pieces[c * per_chunk:(c + 1) * per_chunk]:
                piece()

    @pl.when(s % 2 == 0)
    def _():
        step(raw_a, raw_b)

    @pl.when(s % 2 == 1)
    def _():
        step(raw_b, raw_a)


def _xbc(x2, sc, sh, g, w_all, w_off, cw, cb, seq):
    t, d = x2.shape
    n = cw.shape[1]
    taps = cw.shape[0]
    halo = SUBLANES
    tm = _row_tile(seq, 1024)
    tn = _row_tile(n, 512)
    strip = _row_tile(tn, 256)
    rb = 32
    assert w_off % tn == 0
    per_b = seq // tm
    ncol = n // tn
    ntiles = (t // tm) * ncol
    per = tm // halo
    nh = t // halo

    def row_tile(s):
        return jnp.minimum(s, ntiles - 1) // ncol

    def col_tile(s):
        return jnp.minimum(s, ntiles - 1) % ncol

    def out_map(s):
        prev = jnp.maximum(s - 1, 0)
        return (prev // ncol, prev % ncol)

    return pl.pallas_call(
        functools.partial(_xbc_kernel, taps, halo, tm, ncol, per_b, strip, rb, _row_tile(d, 256)),
        grid=(ntiles + 1,),
        in_specs=[pl.BlockSpec((halo, d), lambda s: (jnp.maximum(row_tile(s) * per - 1, 0), 0)),
                  pl.BlockSpec((tm, d), lambda s: (row_tile(s), 0)),
                  pl.BlockSpec((halo, d), lambda s: (jnp.minimum((row_tile(s) + 1) * per, nh - 1), 0)),
                  pl.BlockSpec((1, 1, d), lambda s: (row_tile(s) // per_b, 0, 0)),
                  pl.BlockSpec((1, 1, d), lambda s: (row_tile(s) // per_b, 0, 0)),
                  pl.BlockSpec((1, d), lambda s: (0, 0)),
                  pl.BlockSpec((d, tn), lambda s: (0, w_off // tn + col_tile(s))),
                  pl.BlockSpec((taps, tn), lambda s: (0, out_map(s)[1])),
                  pl.BlockSpec((1, tn), lambda s: (0, out_map(s)[1]))],
        out_specs=pl.BlockSpec((tm, tn), out_map),
        out_shape=jax.ShapeDtypeStruct((t, n), F32),
        scratch_shapes=[pltpu.VMEM((tm + 2 * halo, d), BF16),
                        pltpu.VMEM((tm + 2 * halo, tn), F32),
                        pltpu.VMEM((tm + 2 * halo, tn), F32),
                        pltpu.VMEM((min(taps, SUBLANES) - 1, tm + 2 * halo - SUBLANES, strip), F32)],
        compiler_params=_cp("arbitrary"),
        name="xbc",
    )(x2, x2, x2, sc, sh, g, w_all, cw, cb.reshape(1, n))


def _ssd_kernel(reverse, n_heads, hpg, pdim, nb, *refs):
    if reverse:
        (xs_ref, b_ref, c_ref, dt_ref, bias_ref, alog_ref, yf_ref, z_ref, gn_ref,
         o_ref, h_ref, dt_t, cs_t, wend_t, cdec_t) = refs
    else:
        (xs_ref, b_ref, c_ref, dt_ref, bias_ref, alog_ref, dskip_ref,
         o_ref, h_ref, dt_t, cs_t, wend_t, cdec_t) = refs
    ln = CHUNK
    w = hpg * pdim
    hd2 = 2 * n_heads
    nt = (((1,), (1,)), ((), ()))

    @pl.when(pl.program_id(1) == 0)
    def _():
        h_ref[...] = jnp.zeros_like(h_ref)

    ri = lax.broadcasted_iota(jnp.int32, (ln, ln), 0)
    ci = lax.broadcasted_iota(jnp.int32, (ln, ln), 1)
    mask = (ci >= ri) if reverse else (ci <= ri)
    cum = mask.astype(BF16)
    edge = 0 if reverse else ln - 1

    cs_alls = []
    for bi in range(nb):
        dt_all = _softplus(dt_ref[bi] + bias_ref[...])
        a_all = dt_all * (-jnp.exp(alog_ref[...]) * LOG2E)
        a_hi = a_all.astype(BF16)
        r1 = a_all - a_hi.astype(F32)
        a_mid = r1.astype(BF16)
        a_lo = (r1 - a_mid.astype(F32)).astype(BF16)
        cs_all = _bdot(cum, a_hi) + _bdot(cum, a_mid) + _bdot(cum, a_lo)
        cs_edge = cs_all[edge:edge + 1, :]
        dt_t[bi] = dt_all.T
        cs_t[bi] = cs_all.T
        wend_t[bi] = (dt_all * jnp.exp2(cs_edge - cs_all)).T
        cdec_t[bi] = jnp.broadcast_to(jnp.exp2(cs_edge), (ln, hd2)).T
        cs_alls.append(cs_all)

    dirbase = n_heads if reverse else 0
    def group_body(g, bi):
        cs_all = cs_alls[bi]
        lanes = slice(g * w, (g + 1) * w)
        xs = xs_ref[bi, :, lanes]
        x_t = xs.T
        bgb = b_ref[bi, :, g * D_STATE:(g + 1) * D_STATE].astype(BF16)
        cgb = c_ref[bi, :, g * D_STATE:(g + 1) * D_STATE].astype(BF16)
        cb = lax.dot_general(cgb, bgb, nt, preferred_element_type=F32)
        h_in = h_ref[bi, g]
        yoff = lax.dot_general(cgb, h_in.astype(BF16), nt, preferred_element_type=F32)
        y_parts, xw_parts, cdec_parts = [], [], []
        yield
        for jp in range(hpg // 2):
            ms, xdts, cols = [], [], []
            for j in (2 * jp, 2 * jp + 1):
                c = dirbase + g * hpg + j
                rows = slice(j * pdim, (j + 1) * pdim)
                xdts.append((x_t[rows] * dt_t[bi, c:c + 1, :]).astype(BF16))
                xw_parts.append((x_t[rows] * wend_t[bi, c:c + 1, :]).astype(BF16))
                col = jnp.broadcast_to(cs_all[:, c:c + 1], (ln, ln))
                seg = col - cs_t[bi, c:c + 1, :]
                ms.append((cb * jnp.where(mask, jnp.exp2(seg), 0.0)).astype(BF16))
                cols.append(col)
                cdec_parts.append(jnp.broadcast_to(cdec_t[bi, c:c + 1, :], (pdim, D_STATE)))
            zero = jnp.zeros((pdim, ln), BF16)
            xdt_bd = jnp.concatenate([jnp.concatenate([xdts[0], zero], axis=1),
                                      jnp.concatenate([zero, xdts[1]], axis=1)], axis=0)
            yd = lax.dot_general(jnp.concatenate(ms, axis=1), xdt_bd, nt,
                                 preferred_element_type=F32)
            din = jnp.exp2(jnp.where(ci < pdim, cols[0], cols[1]))
            y_parts.append(yd + yoff[:, 2 * jp * pdim:(2 * jp + 2) * pdim] * din)
            yield
        y = jnp.concatenate(y_parts, axis=1)
        st = _bdot(jnp.concatenate(xw_parts, axis=0), bgb)
        h_ref[bi, g] = h_in * jnp.concatenate(cdec_parts, axis=0) + st
        if reverse:
            y = (y + yf_ref[bi, :, lanes]) * _silu(z_ref[bi, :, lanes])
            y = y * lax.rsqrt(jnp.mean(y * y, axis=-1, keepdims=True) + EPS) * gn_ref[:, lanes]
            o_ref[bi, :, lanes] = y.astype(o_ref.dtype)
        else:
            o_ref[bi, :, lanes] = y + dskip_ref[:, lanes] * xs

    for g in range(N_GROUPS):
        running = [group_body(g, bi) for bi in range(nb)]
        while running:
            running = [gen for gen in running if next(gen, "done") != "done"]


def _ssd(reverse, xbc3, dt3, bias_row, alog_row, extra, n_heads, d_ssm):
    bsz, seq, _ = xbc3.shape
    hd2 = 2 * n_heads
    hpg = n_heads // N_GROUPS
    pdim = d_ssm // n_heads
    w = hpg * pdim
    nc = seq // CHUNK
    ln = CHUNK
    gn = N_GROUPS * D_STATE
    assert CHUNK == D_STATE and d_ssm % gn == 0 and 2 * pdim == CHUNK and hpg % 2 == 0
    nb = 2 if bsz % 2 == 0 else 1

    def chunk(b, z):
        return (b, nc - 1 - z if reverse else z)

    in_specs = [pl.BlockSpec((nb, ln, d_ssm), lambda b, z: (*chunk(b, z), 0)),
                pl.BlockSpec((nb, ln, gn), lambda b, z: (*chunk(b, z), d_ssm // gn)),
                pl.BlockSpec((nb, ln, gn), lambda b, z: (*chunk(b, z), d_ssm // gn + 1)),
                pl.BlockSpec((nb, ln, hd2), lambda b, z: (*chunk(b, z), 0)),
                pl.BlockSpec((1, hd2), lambda b, z: (0, 0)),
                pl.BlockSpec((1, hd2), lambda b, z: (0, 0))]
    args = [xbc3, xbc3, xbc3, dt3, bias_row, alog_row]
    if reverse:
        y_f, proj3, gnorm = extra
        in_specs += [pl.BlockSpec((nb, ln, d_ssm), lambda b, z: (*chunk(b, z), 0)),
                     pl.BlockSpec((nb, ln, d_ssm), lambda b, z: (*chunk(b, z), 0)),
                     pl.BlockSpec((1, d_ssm), lambda b, z: (0, 0))]
        args += [y_f, proj3, gnorm]
        out_dtype = BF16
    else:
        (dskip,) = extra
        in_specs += [pl.BlockSpec((1, d_ssm), lambda b, z: (0, 0))]
        args += [dskip]
        out_dtype = F32
    table = pltpu.VMEM((nb, hd2, ln), F32)
    return pl.pallas_call(
        functools.partial(_ssd_kernel, reverse, n_heads, hpg, pdim, nb),
        grid=(bsz // nb, nc),
        in_specs=in_specs,
        out_specs=pl.BlockSpec((nb, ln, d_ssm), lambda b, z: (*chunk(b, z), 0)),
        out_shape=jax.ShapeDtypeStruct((bsz, seq, d_ssm), out_dtype),
        scratch_shapes=[pltpu.VMEM((nb, N_GROUPS, w, D_STATE), F32), table, table, table, table],
        compiler_params=_cp("arbitrary", "arbitrary"),
        name="ssd_bwd" if reverse else "ssd_fwd",
    )(*args)


def _conf_kernel(taps, halo, ts, rb, lc, pa_ref, ma_ref, na_ref, pb_ref, mb_ref, nb_ref,
                 bga_ref, bgb_ref, w_ref, bdw_ref, lng_ref, lnb_ref, o_ref, ext_ref, shift_ref, conv_ref):
    d = o_ref.shape[-1]

    def glu(a, b):
        return (a + bga_ref[...]) * jax.nn.sigmoid(b + bgb_ref[...])

    _fill_ext(ext_ref, glu(pa_ref[0], pb_ref[0]), glu(ma_ref[0], mb_ref[0]),
              glu(na_ref[0], nb_ref[0]), halo, ts)
    nrows = shift_ref.shape[1]
    for q in range(SUBLANES):
        shift_ref[q] = ext_ref[q:q + nrows, :]
    first = halo - (taps - 1) // 2

    for c0 in range(0, d, lc):
        lanes = slice(c0, c0 + lc)
        nblk = ts // SUBLANES
        acc = [jnp.broadcast_to(bdw_ref[:, lanes], (SUBLANES, lc))] * nblk
        for q in range(SUBLANES):
            ms = [(first + k) // SUBLANES for k in range(taps) if (first + k) % SUBLANES == q]
            w_b = {m: jnp.broadcast_to(w_ref[SUBLANES * m + q - first:SUBLANES * m + q - first + 1, lanes],
                                       (SUBLANES, lc)) for m in ms}
            for i in range(nblk + max(ms)):
                blk = shift_ref[q, SUBLANES * i:SUBLANES * (i + 1), lanes]
                for m in ms:
                    if 0 <= i - m < nblk:
                        acc[i - m] = acc[i - m] + blk * w_b[m]
        for i in range(nblk):
            conv_ref[SUBLANES * i:SUBLANES * (i + 1), lanes] = acc[i]

    u = conv_ref[...]
    uc = u - jnp.mean(u, axis=-1, keepdims=True)
    y = uc * lax.rsqrt(jnp.mean(uc * uc, axis=-1, keepdims=True) + EPS)
    y = y * lng_ref[...] + lnb_ref[...]
    o_ref[0] = _silu(y).astype(o_ref.dtype)


def _conf(proj3, col_off, b_glu, w_dw, b_dw, ln_g, ln_b):
    bsz, seq, _ = proj3.shape
    taps, d = w_dw.shape
    halo = 16
    ts = _row_tile(seq, 256)
    rb = 64
    lc = 128
    ablk = col_off // d
    specs_a = _halo_specs(ts, halo, seq, d, lambda: ablk)
    specs_b = _halo_specs(ts, halo, seq, d, lambda: ablk + 1)
    row = lambda b_, s: (0, 0)
    return pl.pallas_call(
        functools.partial(_conf_kernel, taps, halo, ts, rb, lc),
        grid=(bsz, seq // ts),
        in_specs=specs_a + specs_b + [pl.BlockSpec((1, d), row), pl.BlockSpec((1, d), row),
                                      pl.BlockSpec((taps, d), row), pl.BlockSpec((1, d), row),
                                      pl.BlockSpec((1, d), row), pl.BlockSpec((1, d), row)],
        out_specs=pl.BlockSpec((1, ts, d), lambda b_, s: (b_, s, 0)),
        out_shape=jax.ShapeDtypeStruct((bsz, seq, d), BF16),
        scratch_shapes=[pltpu.VMEM((ts + 2 * halo, d), F32),
                        pltpu.VMEM((SUBLANES, ts + 2 * halo - SUBLANES, d), F32),
                        pltpu.VMEM((ts, d), F32)],
        compiler_params=_cp("arbitrary", "arbitrary"),
        name="conf",
    )(proj3, proj3, proj3, proj3, proj3, proj3,
      b_glu[:d].reshape(1, d), b_glu[d:].reshape(1, d), w_dw, b_dw.reshape(1, d),
      ln_g.reshape(1, d), ln_b.reshape(1, d))


def _merge_kernel(ya_ref, u_ref, wa_ref, wb_ref, bb_ref, ga_ref, gb_ref, bga_ref, bgb_ref, o_ref):
    y_a = _bdot(ya_ref[...], wa_ref[...])
    y_b = _bdot(u_ref[...], wb_ref[...]) + bb_ref[...]
    g_a = jax.nn.sigmoid(ga_ref[...] + bga_ref[...])
    g_b = jax.nn.sigmoid(gb_ref[...] + bgb_ref[...])
    o_ref[...] = (g_a * y_a + g_b * y_b).astype(o_ref.dtype)


def _merge(y_ssd, u, w_ssm_out, w_conv_out, b_conv_out, proj2, gate_off, b_gate):
    t, d_ssm = y_ssd.shape
    d_conv = u.shape[1]
    d = w_ssm_out.shape[1]
    tm = _row_tile(t, 1024)
    tn = _row_tile(d, 512)
    ga = gate_off // tn
    gb = (gate_off + d) // tn
    bg = b_gate.reshape(1, 2 * d)
    return pl.pallas_call(
        _merge_kernel,
        grid=(t // tm, d // tn),
        in_specs=[pl.BlockSpec((tm, d_ssm), lambda i, j: (i, 0)),
                  pl.BlockSpec((tm, d_conv), lambda i, j: (i, 0)),
                  pl.BlockSpec((d_ssm, tn), lambda i, j: (0, j)),
                  pl.BlockSpec((d_conv, tn), lambda i, j: (0, j)),
                  pl.BlockSpec((1, tn), lambda i, j: (0, j)),
                  pl.BlockSpec((tm, tn), lambda i, j: (i, ga + j)),
                  pl.BlockSpec((tm, tn), lambda i, j: (i, gb + j)),
                  pl.BlockSpec((1, tn), lambda i, j: (0, j)),
                  pl.BlockSpec((1, tn), lambda i, j: (0, d // tn + j))],
        out_specs=pl.BlockSpec((tm, tn), lambda i, j: (i, j)),
        out_shape=jax.ShapeDtypeStruct((t, d), BF16),
        compiler_params=_cp("arbitrary", "arbitrary"),
        name="merge",
    )(y_ssd, u, w_ssm_out, w_conv_out, b_conv_out.reshape(1, d), proj2, proj2, bg, bg)


def _mixout_kernel(m_ref, w_ref, x_ref, gpost_ref, g1_ref, gpre_ref, sc2_ref, sh2_ref, x1_ref, h2_ref):
    mix = _bdot(m_ref[...], w_ref[...])
    nm = mix * lax.rsqrt(jnp.mean(mix * mix, axis=-1, keepdims=True) + EPS) * gpost_ref[...]
    x1 = x_ref[...] + g1_ref[0] * nm
    x1_ref[...] = x1
    y = x1 * lax.rsqrt(jnp.mean(x1 * x1, axis=-1, keepdims=True) + EPS) * gpre_ref[...]
    h2_ref[...] = (y * (1.0 + sc2_ref[0]) + sh2_ref[0]).astype(h2_ref.dtype)


def _mixout(mix_in, w_mix_out, x2, g_post, g1, g_pre_ffn, sc2, sh2, seq):
    t, d = x2.shape
    tm = _row_tile(seq, 512)
    per_b = seq // tm
    row = lambda i: (0, 0)
    brow = lambda i: (i // per_b, 0, 0)
    return pl.pallas_call(
        _mixout_kernel,
        grid=(t // tm,),
        in_specs=[pl.BlockSpec((tm, d), lambda i: (i, 0)),
                  pl.BlockSpec((d, d), row),
                  pl.BlockSpec((tm, d), lambda i: (i, 0)),
                  pl.BlockSpec((1, d), row),
                  pl.BlockSpec((1, 1, d), brow),
                  pl.BlockSpec((1, d), row),
                  pl.BlockSpec((1, 1, d), brow),
                  pl.BlockSpec((1, 1, d), brow)],
        out_specs=[pl.BlockSpec((tm, d), lambda i: (i, 0)),
                   pl.BlockSpec((tm, d), lambda i: (i, 0))],
        out_shape=[jax.ShapeDtypeStruct((t, d), F32), jax.ShapeDtypeStruct((t, d), BF16)],
        compiler_params=_cp("arbitrary"),
        name="mixout",
    )(mix_in, w_mix_out, x2, g_post, g1, g_pre_ffn, sc2, sh2)


def _ffn_up_kernel(h_ref, wg_ref, wu_ref, o_ref):
    h = h_ref[...]
    o_ref[...] = (_silu(_bdot(h, wg_ref[...])) * _bdot(h, wu_ref[...])).astype(o_ref.dtype)


def _ffn_up(h2, w_gate_up):
    t, d = h2.shape
    d_ff = w_gate_up.shape[1] // 2
    tm = _row_tile(t, 1024)
    tn = _row_tile(d_ff, 512)
    nj = d_ff // tn
    return pl.pallas_call(
        _ffn_up_kernel,
        grid=(t // tm, nj),
        in_specs=[pl.BlockSpec((tm, d), lambda i, j: (i, 0)),
                  pl.BlockSpec((d, tn), lambda i, j: (0, j)),
                  pl.BlockSpec((d, tn), lambda i, j: (0, nj + j))],
        out_specs=pl.BlockSpec((tm, tn), lambda i, j: (i, j)),
        out_shape=jax.ShapeDtypeStruct((t, d_ff), BF16),
        compiler_params=_cp("arbitrary", "arbitrary"),
        name="ffn_up",
    )(h2, w_gate_up, w_gate_up)


def _ffn_down_kernel(a_ref, w_ref, x1_ref, gpost_ref, g2_ref, o_ref):
    f = _bdot(a_ref[...], w_ref[...])
    nf = f * lax.rsqrt(jnp.mean(f * f, axis=-1, keepdims=True) + EPS) * gpost_ref[...]
    o_ref[...] = x1_ref[...] + g2_ref[0] * nf


def _ffn_down(act, w_down, x1, g_post, g2, seq):
    t, d_ff = act.shape
    d = w_down.shape[1]
    tm = _row_tile(seq, 512)
    per_b = seq // tm
    return pl.pallas_call(
        _ffn_down_kernel,
        grid=(t // tm,),
        in_specs=[pl.BlockSpec((tm, d_ff), lambda i: (i, 0)),
                  pl.BlockSpec((d_ff, d), lambda i: (0, 0), pipeline_mode=pl.Buffered(1)),
                  pl.BlockSpec((tm, d), lambda i: (i, 0)),
                  pl.BlockSpec((1, d), lambda i: (0, 0)),
                  pl.BlockSpec((1, 1, d), lambda i: (i // per_b, 0, 0))],
        out_specs=pl.BlockSpec((tm, d), lambda i: (i, 0)),
        out_shape=jax.ShapeDtypeStruct((t, d), F32),
        compiler_params=_cp("arbitrary"),
        name="ffn_down",
    )(act, w_down, x1, g_post, g2)


def kernel(x, c, w_ada, b_ada, g_pre_mix, g_post_mix, w_in, w_conv_ssm, b_conv_ssm, dt_bias_fwd, dt_bias_bwd, a_log_fwd, a_log_bwd, d_skip, g_ssm_norm, w_ssm_out, b_glu, w_dw, b_dw, ln_g, ln_b, w_conv_out, b_conv_out, b_gate, w_mix_out, g_pre_ffn, g_post_ffn, w_gate_up, w_down):
    bsz, seq, d = x.shape
    depth = w_ada.shape[0]
    n_heads = dt_bias_fwd.shape[1]
    d_ssm = w_ssm_out.shape[1]
    d_xbc = w_conv_ssm.shape[2]
    d_conv = w_dw.shape[2]
    pdim = d_ssm // n_heads
    assert d_xbc == d_ssm + 2 * N_GROUPS * D_STATE and seq % CHUNK == 0 and n_heads % N_GROUPS == 0
    s1, s2, s3 = d_ssm, d_ssm + d_xbc, d_ssm + d_xbc + 2 * n_heads
    glu_off, gate_off = d_ssm, d_ssm + 2 * d_conv

    x2 = x.reshape(bsz * seq, d)
    for l in range(depth):
        w_all = w_in[l].astype(BF16)
        w_zgg = jnp.concatenate([w_all[:, :s1], w_all[:, s3:]], axis=1)
        row = lambda v: v.reshape(1, -1)

        mod = _ada(c, w_ada[l], b_ada[l])
        sh1, sc1, g1, sh2, sc2, g2 = [m.reshape(bsz, 1, d) for m in jnp.split(mod, 6, axis=-1)]

        proj2, dt2 = _inproj(x2, sc1, sh1, row(g_pre_mix[l]), w_zgg, w_all, s2, s3 - s2, seq)
        proj3 = proj2.reshape(bsz, seq, -1)
        dt3 = dt2.reshape(bsz, seq, -1)

        xbc3 = _xbc(x2, sc1, sh1, row(g_pre_mix[l]), w_all, s1, w_conv_ssm[l], b_conv_ssm[l],
                    seq).reshape(bsz, seq, d_xbc)
        bias_row = row(jnp.concatenate([dt_bias_fwd[l], dt_bias_bwd[l]]))
        alog_row = row(jnp.concatenate([a_log_fwd[l], a_log_bwd[l]]))
        dskip_row = row(jnp.repeat(d_skip[l], pdim))
        y_f = _ssd(False, xbc3, dt3, bias_row, alog_row, (dskip_row,), n_heads, d_ssm)
        y_ssd = _ssd(True, xbc3, dt3, bias_row, alog_row, (y_f, proj3, row(g_ssm_norm[l])), n_heads, d_ssm)

        u = _conf(proj3, glu_off, b_glu[l], w_dw[l], b_dw[l], ln_g[l], ln_b[l])

        mix_in = _merge(y_ssd.reshape(bsz * seq, d_ssm), u.reshape(bsz * seq, d_conv),
                        w_ssm_out[l].astype(BF16), w_conv_out[l].astype(BF16), b_conv_out[l],
                        proj2, gate_off, b_gate[l])
        x1, h2 = _mixout(mix_in, w_mix_out[l].astype(BF16), x2, row(g_post_mix[l]), g1,
                         row(g_pre_ffn[l]), sc2, sh2, seq)
        act = _ffn_up(h2, w_gate_up[l].astype(BF16))
        x2 = _ffn_down(act, w_down[l].astype(BF16), x1, row(g_post_ffn[l]), g2, seq)
    return x2.reshape(bsz, seq, d)
```

```python
import functools

import jax
import jax.numpy as jnp
from jax import lax
from jax.experimental import pallas as pl
from jax.experimental.pallas import tpu as pltpu

F32 = jnp.float32
BF16 = jnp.bfloat16

EPS = 1e-6
N_GROUPS = 8
D_STATE = 128
CHUNK = 128
GROUPS_IN_FLIGHT_FORWARD = 1
GROUPS_IN_FLIGHT_REVERSE = 2
V7X_VMEM_LIMIT_BYTES = 58 * 1024 * 1024
SUBLANES = 8
LANES = 128
LOG2E = 1.4426950408889634


def _cp(*sem):
    return pltpu.CompilerParams(dimension_semantics=sem, vmem_limit_bytes=V7X_VMEM_LIMIT_BYTES)


def _silu(v):
    return v * jax.nn.sigmoid(v)


def _softplus(v):
    return jnp.maximum(v, 0.0) + jnp.log1p(jnp.exp(-jnp.abs(v)))


def _bdot(a, b):
    return jnp.dot(a, b, preferred_element_type=F32)


def _row_tile(n, want):
    t = min(n, want)
    assert n % t == 0, (n, t)
    return t


def _lane_tile(n, want):
    return max(t for t in range(LANES, min(n, want) + 1, LANES) if n % t == 0)


def _ada_kernel(c_ref, w_ref, b_ref, o_ref):
    ca = _silu(c_ref[...]).astype(BF16)
    o_ref[...] = _bdot(ca, w_ref[...].astype(BF16)) + b_ref[...]


def _ada(c, w, b):
    bsz, d = c.shape
    n = w.shape[1]
    rows = 16
    tn = _row_tile(n, 1024)
    cpad = jnp.zeros((rows, d), F32).at[:bsz].set(c)
    out = pl.pallas_call(
        _ada_kernel,
        grid=(n // tn,),
        in_specs=[pl.BlockSpec((rows, d), lambda j: (0, 0)),
                  pl.BlockSpec((d, tn), lambda j: (0, j)),
                  pl.BlockSpec((1, tn), lambda j: (0, j))],
        out_specs=pl.BlockSpec((rows, tn), lambda j: (0, j)),
        out_shape=jax.ShapeDtypeStruct((rows, n), F32),
        compiler_params=_cp("arbitrary"),
        name="ada",
    )(cpad, w, b.reshape(1, n))
    return out[:bsz]


def _inproj_kernel(x_ref, sc_ref, sh_ref, g_ref, w_ref, wdt_ref, o_ref, odt_ref, h_ref):
    @pl.when(pl.program_id(1) == 0)
    def _():
        x = x_ref[...]
        y = x * lax.rsqrt(jnp.mean(x * x, axis=-1, keepdims=True) + EPS) * g_ref[...]
        h = (y * (1.0 + sc_ref[0]) + sh_ref[0]).astype(BF16)
        h_ref[...] = h
        odt_ref[...] = _bdot(h, wdt_ref[...])

    o_ref[...] = _bdot(h_ref[...], w_ref[...])


def _inproj(x2, sc, sh, g, w_main, w_all, dt_off, ndt, seq):
    t, d = x2.shape
    n = w_main.shape[1]
    tm = _row_tile(seq, 1024)
    tn = _row_tile(n, 1024)
    ncol = n // tn
    assert dt_off % ndt == 0
    per_b = seq // tm
    return pl.pallas_call(
        _inproj_kernel,
        grid=(t // tm, ncol),
        in_specs=[pl.BlockSpec((tm, d), lambda i, j: (i, 0)),
                  pl.BlockSpec((1, 1, d), lambda i, j: (i // per_b, 0, 0)),
                  pl.BlockSpec((1, 1, d), lambda i, j: (i // per_b, 0, 0)),
                  pl.BlockSpec((1, d), lambda i, j: (0, 0)),
                  pl.BlockSpec((d, tn), lambda i, j: (0, j)),
                  pl.BlockSpec((d, ndt), lambda i, j: (0, dt_off // ndt))],
        out_specs=[pl.BlockSpec((tm, tn), lambda i, j: (i, j)),
                   pl.BlockSpec((tm, ndt), lambda i, j: (i, 0))],
        out_shape=[jax.ShapeDtypeStruct((t, ncol * tn), F32),
                   jax.ShapeDtypeStruct((t, ndt), F32)],
        scratch_shapes=[pltpu.VMEM((tm, d), BF16)],
        compiler_params=_cp("arbitrary", "arbitrary"),
        name="inproj",
    )(x2, sc, sh, g, w_main, w_all)


def _fill_ext(ext_ref, prev, main, nxt, halo, ts):
    s = pl.program_id(1)
    last = pl.num_programs(1) - 1
    ext_ref[0:halo, :] = jnp.where(s > 0, prev, 0.0)
    ext_ref[halo:halo + ts, :] = main
    ext_ref[halo + ts:halo + ts + halo, :] = jnp.where(s < last, nxt, 0.0)


def _dwconv_rows(tap_rows, w_ref, b_ref, rb, lanes, taps):
    acc = jnp.broadcast_to(b_ref[:, lanes], (rb, lanes.stop - lanes.start))
    for k in range(taps):
        acc = acc + tap_rows(k) * w_ref[k:k + 1, lanes]
    return acc


def _halo_specs(ts, halo, seq, width, col_block):
    per = ts // halo
    nh = seq // halo

    def prev_map(b, s, *c):
        return (b, jnp.maximum(s * per - 1, 0), col_block(*c))

    def main_map(b, s, *c):
        return (b, s, col_block(*c))

    def next_map(b, s, *c):
        return (b, jnp.minimum((s + 1) * per, nh - 1), col_block(*c))

    return [pl.BlockSpec((1, halo, width), prev_map),
            pl.BlockSpec((1, ts, width), main_map),
            pl.BlockSpec((1, halo, width), next_map)]


def _xbc_kernel(taps, halo, tm, ncol, per_b, strip, rb, kc,
                xp_ref, xm_ref, xn_ref, sc_ref, sh_ref, g_ref, w_ref, cw_ref, cb_ref,
                o_ref, h_ref, raw_a, raw_b, shift_ref):
    s = pl.program_id(0)
    ntiles = pl.num_programs(0) - 1
    cur = jnp.minimum(s, ntiles - 1)
    tn = o_ref.shape[-1]
    first = halo - (taps - 1) // 2
    shifts = sorted({(first + k) % SUBLANES for k in range(taps)} - {0})
    nrows = shift_ref.shape[1]

    @pl.when(s == 0)
    def _():
        raw_b[...] = jnp.zeros_like(raw_b)

    @pl.when((s < ntiles) & (cur % ncol == 0))
    def _():
        def modulated(x):
            y = x * lax.rsqrt(jnp.mean(x * x, axis=-1, keepdims=True) + EPS) * g_ref[...]
            return y * (1.0 + sc_ref[0]) + sh_ref[0]

        h_ref[...] = jnp.concatenate(
            [modulated(xp_ref[...]), modulated(xm_ref[...]), modulated(xn_ref[...])], axis=0).astype(BF16)

    def step(raw_w, raw_r):
        pos = (jnp.maximum(s - 1, 0) // ncol) % per_b
        raw_r[0:halo, :] = jnp.where(pos > 0, raw_r[0:halo, :], 0.0)
        raw_r[halo + tm:, :] = jnp.where(pos < per_b - 1, raw_r[halo + tm:, :], 0.0)
        pieces = []
        for c0 in range(0, tn, strip):
            lanes = slice(c0, c0 + strip)

            def copy_shifts(lanes=lanes):
                for n, q in enumerate(shifts):
                    shift_ref[n] = raw_r[q:q + nrows, lanes]

            def tap_rows(r0, k, lanes=lanes):
                m, q = divmod(first + k, SUBLANES)
                rows = slice(r0 + SUBLANES * m, r0 + SUBLANES * m + rb)
                return raw_r[rows, lanes] if q == 0 else shift_ref[shifts.index(q), rows, :]

            def conv_rows(r0, lanes=lanes, tap_rows=tap_rows):
                acc = _dwconv_rows(functools.partial(tap_rows, r0), cw_ref, cb_ref, rb, lanes, taps)
                o_ref[r0:r0 + rb, lanes] = _silu(acc)

            pieces.append(copy_shifts)
            pieces += [functools.partial(conv_rows, r0) for r0 in range(0, tm, rb)]
        nk = h_ref.shape[1] // kc
        per_chunk = -(-len(pieces) // nk)
        for c in range(nk):
            part = _bdot(h_ref[:, c * kc:(c + 1) * kc], w_ref[c * kc:(c + 1) * kc, :])
            if c == 0:
                raw_w[...] = part
            else:
                raw_w[...] += part
            for piece in pieces[c * per_chunk:(c + 1) * per_chunk]:
                piece()

    @pl.when(s % 2 == 0)
    def _():
        step(raw_a, raw_b)

    @pl.when(s % 2 == 1)
    def _():
        step(raw_b, raw_a)


def _xbc(x2, sc, sh, g, w_all, w_off, cw, cb, seq):
    t, d = x2.shape
    n = cw.shape[1]
    taps = cw.shape[0]
    halo = SUBLANES
    tm = _row_tile(seq, 1024)
    tn = _row_tile(n, 512)
    strip = _row_tile(tn, 256)
    rb = 32
    assert w_off % tn == 0
    per_b = seq // tm
    ncol = n // tn
    ntiles = (t // tm) * ncol
    per = tm // halo
    nh = t // halo

    def row_tile(s):
        return jnp.minimum(s, ntiles - 1) // ncol

    def col_tile(s):
        return jnp.minimum(s, ntiles - 1) % ncol

    def out_map(s):
        prev = jnp.maximum(s - 1, 0)
        return (prev // ncol, prev % ncol)

    return pl.pallas_call(
        functools.partial(_xbc_kernel, taps, halo, tm, ncol, per_b, strip, rb, _row_tile(d, 256)),
        grid=(ntiles + 1,),
        in_specs=[pl.BlockSpec((halo, d), lambda s: (jnp.maximum(row_tile(s) * per - 1, 0), 0)),
                  pl.BlockSpec((tm, d), lambda s: (row_tile(s), 0)),
                  pl.BlockSpec((halo, d), lambda s: (jnp.minimum((row_tile(s) + 1) * per, nh - 1), 0)),
                  pl.BlockSpec((1, 1, d), lambda s: (row_tile(s) // per_b, 0, 0)),
                  pl.BlockSpec((1, 1, d), lambda s: (row_tile(s) // per_b, 0, 0)),
                  pl.BlockSpec((1, d), lambda s: (0, 0)),
                  pl.BlockSpec((d, tn), lambda s: (0, w_off // tn + col_tile(s))),
                  pl.BlockSpec((taps, tn), lambda s: (0, out_map(s)[1])),
                  pl.BlockSpec((1, tn), lambda s: (0, out_map(s)[1]))],
        out_specs=pl.BlockSpec((tm, tn), out_map),
        out_shape=jax.ShapeDtypeStruct((t, n), F32),
        scratch_shapes=[pltpu.VMEM((tm + 2 * halo, d), BF16),
                        pltpu.VMEM((tm + 2 * halo, tn), F32),
                        pltpu.VMEM((tm + 2 * halo, tn), F32),
                        pltpu.VMEM((min(taps, SUBLANES) - 1, tm + 2 * halo - SUBLANES, strip), F32)],
        compiler_params=_cp("arbitrary"),
        name="xbc",
    )(x2, x2, x2, sc, sh, g, w_all, cw, cb.reshape(1, n))


def _ssd_kernel(reverse, n_heads, hpg, pdim, nb, *refs):
    if reverse:
        (xs_ref, b_ref, c_ref, dt_ref, bias_ref, alog_ref, yf_ref, z_ref, gn_ref,
         o_ref, h_ref, dt_t, cs_t, wend_t, cdec_t) = refs
    else:
        (xs_ref, b_ref, c_ref, dt_ref, bias_ref, alog_ref, dskip_ref,
         o_ref, h_ref, dt_t, cs_t, wend_t, cdec_t) = refs
    ln = CHUNK
    w = hpg * pdim
    hd2 = 2 * n_heads
    nt = (((1,), (1,)), ((), ()))

    @pl.when(pl.program_id(1) == 0)
    def _():
        h_ref[...] = jnp.zeros_like(h_ref)

    ri = lax.broadcasted_iota(jnp.int32, (ln, ln), 0)
    ci = lax.broadcasted_iota(jnp.int32, (ln, ln), 1)
    mask = (ci >= ri) if reverse else (ci <= ri)
    cum = mask.astype(BF16)
    edge = 0 if reverse else ln - 1

    cs_alls = []
    for bi in range(nb):
        dt_all = _softplus(dt_ref[bi] + bias_ref[...])
        a_all = dt_all * (-jnp.exp(alog_ref[...]) * LOG2E)
        a_hi = a_all.astype(BF16)
        r1 = a_all - a_hi.astype(F32)
        a_mid = r1.astype(BF16)
        a_lo = (r1 - a_mid.astype(F32)).astype(BF16)
        cs_all = _bdot(cum, a_hi) + _bdot(cum, a_mid) + _bdot(cum, a_lo)
        cs_edge = cs_all[edge:edge + 1, :]
        dt_t[bi] = dt_all.T
        cs_t[bi] = cs_all.T
        wend_t[bi] = (dt_all * jnp.exp2(cs_edge - cs_all)).T
        cdec_t[bi] = jnp.broadcast_to(jnp.exp2(cs_edge), (ln, hd2)).T
        cs_alls.append(cs_all)

    dirbase = n_heads if reverse else 0
    def group_body(g, bi):
        cs_all = cs_alls[bi]
        lanes = slice(g * w, (g + 1) * w)
        xs = xs_ref[bi, :, lanes]
        x_t = xs.T
        bgb = b_ref[bi, :, g * D_STATE:(g + 1) * D_STATE].astype(BF16)
        cgb = c_ref[bi, :, g * D_STATE:(g + 1) * D_STATE].astype(BF16)
        cb = lax.dot_general(cgb, bgb, nt, preferred_element_type=F32)
        h_in = h_ref[bi, g]
        yoff = lax.dot_general(cgb, h_in.astype(BF16), nt, preferred_element_type=F32)
        y_parts, xw_parts, cdec_parts = [], [], []
        yield
        for jp in range(hpg // 2):
            ms, xdts, cols = [], [], []
            for j in (2 * jp, 2 * jp + 1):
                c = dirbase + g * hpg + j
                rows = slice(j * pdim, (j + 1) * pdim)
                xdts.append((x_t[rows] * dt_t[bi, c:c + 1, :]).astype(BF16))
                xw_parts.append((x_t[rows] * wend_t[bi, c:c + 1, :]).astype(BF16))
                col = jnp.broadcast_to(cs_all[:, c:c + 1], (ln, ln))
                seg = col - cs_t[bi, c:c + 1, :]
                ms.append((cb * jnp.where(mask, jnp.exp2(seg), 0.0)).astype(BF16))
                cols.append(col)
                cdec_parts.append(jnp.broadcast_to(cdec_t[bi, c:c + 1, :], (pdim, D_STATE)))
            zero = jnp.zeros((pdim, ln), BF16)
            xdt_bd = jnp.concatenate([jnp.concatenate([xdts[0], zero], axis=1),
                                      jnp.concatenate([zero, xdts[1]], axis=1)], axis=0)
            yd = lax.dot_general(jnp.concatenate(ms, axis=1), xdt_bd, nt,
                                 preferred_element_type=F32)
            din = jnp.exp2(jnp.where(ci < pdim, cols[0], cols[1]))
            y_parts.append(yd + yoff[:, 2 * jp * pdim:(2 * jp + 2) * pdim] * din)
            yield
        y = jnp.concatenate(y_parts, axis=1)
        st = _bdot(jnp.concatenate(xw_parts, axis=0), bgb)
        h_ref[bi, g] = h_in * jnp.concatenate(cdec_parts, axis=0) + st
        if reverse:
            y = (y + yf_ref[bi, :, lanes]) * _silu(z_ref[bi, :, lanes])
            y = y * lax.rsqrt(jnp.mean(y * y, axis=-1, keepdims=True) + EPS) * gn_ref[:, lanes]
            o_ref[bi, :, lanes] = y.astype(o_ref.dtype)
        else:
            o_ref[bi, :, lanes] = y + dskip_ref[:, lanes] * xs

    in_flight = GROUPS_IN_FLIGHT_REVERSE if reverse else GROUPS_IN_FLIGHT_FORWARD
    for g0 in range(0, N_GROUPS, in_flight):
        running = [group_body(g, bi) for g in range(g0, g0 + in_flight) for bi in range(nb)]
        while running:
            running = [gen for gen in running if next(gen, "done") != "done"]


def _ssd(reverse, xbc3, dt3, bias_row, alog_row, extra, n_heads, d_ssm):
    bsz, seq, _ = xbc3.shape
    hd2 = 2 * n_heads
    hpg = n_heads // N_GROUPS
    pdim = d_ssm // n_heads
    w = hpg * pdim
    nc = seq // CHUNK
    ln = CHUNK
    gn = N_GROUPS * D_STATE
    assert CHUNK == D_STATE and d_ssm % gn == 0 and 2 * pdim == CHUNK and hpg % 2 == 0
    nb = 2 if bsz % 2 == 0 else 1

    def chunk(b, z):
        return (b, nc - 1 - z if reverse else z)

    in_specs = [pl.BlockSpec((nb, ln, d_ssm), lambda b, z: (*chunk(b, z), 0)),
                pl.BlockSpec((nb, ln, gn), lambda b, z: (*chunk(b, z), d_ssm // gn)),
                pl.BlockSpec((nb, ln, gn), lambda b, z: (*chunk(b, z), d_ssm // gn + 1)),
                pl.BlockSpec((nb, ln, hd2), lambda b, z: (*chunk(b, z), 0)),
                pl.BlockSpec((1, hd2), lambda b, z: (0, 0)),
                pl.BlockSpec((1, hd2), lambda b, z: (0, 0))]
    args = [xbc3, xbc3, xbc3, dt3, bias_row, alog_row]
    if reverse:
        y_f, proj3, gnorm = extra
        in_specs += [pl.BlockSpec((nb, ln, d_ssm), lambda b, z: (*chunk(b, z), 0)),
                     pl.BlockSpec((nb, ln, d_ssm), lambda b, z: (*chunk(b, z), 0)),
                     pl.BlockSpec((1, d_ssm), lambda b, z: (0, 0))]
        args += [y_f, proj3, gnorm]
        out_dtype = BF16
    else:
        (dskip,) = extra
        in_specs += [pl.BlockSpec((1, d_ssm), lambda b, z: (0, 0))]
        args += [dskip]
        out_dtype = F32
    table = pltpu.VMEM((nb, hd2, ln), F32)
    return pl.pallas_call(
        functools.partial(_ssd_kernel, reverse, n_heads, hpg, pdim, nb),
        grid=(bsz // nb, nc),
        in_specs=in_specs,
        out_specs=pl.BlockSpec((nb, ln, d_ssm), lambda b, z: (*chunk(b, z), 0)),
        out_shape=jax.ShapeDtypeStruct((bsz, seq, d_ssm), out_dtype),
        scratch_shapes=[pltpu.VMEM((nb, N_GROUPS, w, D_STATE), F32), table, table, table, table],
        compiler_params=_cp("arbitrary", "arbitrary"),
        name="ssd_bwd" if reverse else "ssd_fwd",
    )(*args)


def _conf_kernel(taps, halo, ts, rb, lc, pa_ref, ma_ref, na_ref, pb_ref, mb_ref, nb_ref,
                 bga_ref, bgb_ref, w_ref, bdw_ref, lng_ref, lnb_ref, o_ref, ext_ref, shift_ref, conv_ref):
    d = o_ref.shape[-1]

    def glu(a, b):
        return (a + bga_ref[...]) * jax.nn.sigmoid(b + bgb_ref[...])

    _fill_ext(ext_ref, glu(pa_ref[0], pb_ref[0]), glu(ma_ref[0], mb_ref[0]),
              glu(na_ref[0], nb_ref[0]), halo, ts)
    nrows = shift_ref.shape[1]
    for q in range(SUBLANES):
        shift_ref[q] = ext_ref[q:q + nrows, :]
    first = halo - (taps - 1) // 2

    for c0 in range(0, d, lc):
        lanes = slice(c0, c0 + lc)
        nblk = ts // SUBLANES
        acc = [jnp.broadcast_to(bdw_ref[:, lanes], (SUBLANES, lc))] * nblk
        for q in range(SUBLANES):
            ms = [(first + k) // SUBLANES for k in range(taps) if (first + k) % SUBLANES == q]
            w_b = {m: jnp.broadcast_to(w_ref[SUBLANES * m + q - first:SUBLANES * m + q - first + 1, lanes],
                                       (SUBLANES, lc)) for m in ms}
            for i in range(nblk + max(ms)):
                blk = shift_ref[q, SUBLANES * i:SUBLANES * (i + 1), lanes]
                for m in ms:
                    if 0 <= i - m < nblk:
                        acc[i - m] = acc[i - m] + blk * w_b[m]
        for i in range(nblk):
            conv_ref[SUBLANES * i:SUBLANES * (i + 1), lanes] = acc[i]

    u = conv_ref[...]
    uc = u - jnp.mean(u, axis=-1, keepdims=True)
    y = uc * lax.rsqrt(jnp.mean(uc * uc, axis=-1, keepdims=True) + EPS)
    y = y * lng_ref[...] + lnb_ref[...]
    o_ref[0] = _silu(y).astype(o_ref.dtype)


def _conf(proj3, col_off, b_glu, w_dw, b_dw, ln_g, ln_b):
    bsz, seq, _ = proj3.shape
    taps, d = w_dw.shape
    halo = 16
    ts = _row_tile(seq, 256)
    rb = 64
    lc = 128
    ablk = col_off // d
    specs_a = _halo_specs(ts, halo, seq, d, lambda: ablk)
    specs_b = _halo_specs(ts, halo, seq, d, lambda: ablk + 1)
    row = lambda b_, s: (0, 0)
    return pl.pallas_call(
        functools.partial(_conf_kernel, taps, halo, ts, rb, lc),
        grid=(bsz, seq // ts),
        in_specs=specs_a + specs_b + [pl.BlockSpec((1, d), row), pl.BlockSpec((1, d), row),
                                      pl.BlockSpec((taps, d), row), pl.BlockSpec((1, d), row),
                                      pl.BlockSpec((1, d), row), pl.BlockSpec((1, d), row)],
        out_specs=pl.BlockSpec((1, ts, d), lambda b_, s: (b_, s, 0)),
        out_shape=jax.ShapeDtypeStruct((bsz, seq, d), BF16),
        scratch_shapes=[pltpu.VMEM((ts + 2 * halo, d), F32),
                        pltpu.VMEM((SUBLANES, ts + 2 * halo - SUBLANES, d), F32),
                        pltpu.VMEM((ts, d), F32)],
        compiler_params=_cp("arbitrary", "arbitrary"),
        name="conf",
    )(proj3, proj3, proj3, proj3, proj3, proj3,
      b_glu[:d].reshape(1, d), b_glu[d:].reshape(1, d), w_dw, b_dw.reshape(1, d),
      ln_g.reshape(1, d), ln_b.reshape(1, d))


def _merge_kernel(ya_ref, u_ref, wa_ref, wb_ref, bb_ref, ga_ref, gb_ref, bga_ref, bgb_ref, o_ref):
    y_a = _bdot(ya_ref[...], wa_ref[...])
    y_b = _bdot(u_ref[...], wb_ref[...]) + bb_ref[...]
    g_a = jax.nn.sigmoid(ga_ref[...] + bga_ref[...])
    g_b = jax.nn.sigmoid(gb_ref[...] + bgb_ref[...])
    o_ref[...] = (g_a * y_a + g_b * y_b).astype(o_ref.dtype)


def _merge(y_ssd, u, w_ssm_out, w_conv_out, b_conv_out, proj2, gate_off, b_gate):
    t, d_ssm = y_ssd.shape
    d_conv = u.shape[1]
    d = w_ssm_out.shape[1]
    tm = _row_tile(t, 1024)
    tn = _row_tile(d, 512)
    ga = gate_off // tn
    gb = (gate_off + d) // tn
    bg = b_gate.reshape(1, 2 * d)
    return pl.pallas_call(
        _merge_kernel,
        grid=(t // tm, d // tn),
        in_specs=[pl.BlockSpec((tm, d_ssm), lambda i, j: (i, 0)),
                  pl.BlockSpec((tm, d_conv), lambda i, j: (i, 0)),
                  pl.BlockSpec((d_ssm, tn), lambda i, j: (0, j)),
                  pl.BlockSpec((d_conv, tn), lambda i, j: (0, j)),
                  pl.BlockSpec((1, tn), lambda i, j: (0, j)),
                  pl.BlockSpec((tm, tn), lambda i, j: (i, ga + j)),
                  pl.BlockSpec((tm, tn), lambda i, j: (i, gb + j)),
                  pl.BlockSpec((1, tn), lambda i, j: (0, j)),
                  pl.BlockSpec((1, tn), lambda i, j: (0, d // tn + j))],
        out_specs=pl.BlockSpec((tm, tn), lambda i, j: (i, j)),
        out_shape=jax.ShapeDtypeStruct((t, d), BF16),
        compiler_params=_cp("arbitrary", "arbitrary"),
        name="merge",
    )(y_ssd, u, w_ssm_out, w_conv_out, b_conv_out.reshape(1, d), proj2, proj2, bg, bg)


def _mixout_kernel(m_ref, w_ref, x_ref, gpost_ref, g1_ref, gpre_ref, sc2_ref, sh2_ref, x1_ref, h2_ref):
    mix = _bdot(m_ref[...], w_ref[...])
    nm = mix * lax.rsqrt(jnp.mean(mix * mix, axis=-1, keepdims=True) + EPS) * gpost_ref[...]
    x1 = x_ref[...] + g1_ref[0] * nm
    x1_ref[...] = x1
    y = x1 * lax.rsqrt(jnp.mean(x1 * x1, axis=-1, keepdims=True) + EPS) * gpre_ref[...]
    h2_ref[...] = (y * (1.0 + sc2_ref[0]) + sh2_ref[0]).astype(h2_ref.dtype)


def _mixout(mix_in, w_mix_out, x2, g_post, g1, g_pre_ffn, sc2, sh2, seq):
    t, d = x2.shape
    tm = _row_tile(seq, 512)
    per_b = seq // tm
    row = lambda i: (0, 0)
    brow = lambda i: (i // per_b, 0, 0)
    return pl.pallas_call(
        _mixout_kernel,
        grid=(t // tm,),
        in_specs=[pl.BlockSpec((tm, d), lambda i: (i, 0)),
                  pl.BlockSpec((d, d), row),
                  pl.BlockSpec((tm, d), lambda i: (i, 0)),
                  pl.BlockSpec((1, d), row),
                  pl.BlockSpec((1, 1, d), brow),
                  pl.BlockSpec((1, d), row),
                  pl.BlockSpec((1, 1, d), brow),
                  pl.BlockSpec((1, 1, d), brow)],
        out_specs=[pl.BlockSpec((tm, d), lambda i: (i, 0)),
                   pl.BlockSpec((tm, d), lambda i: (i, 0))],
        out_shape=[jax.ShapeDtypeStruct((t, d), F32), jax.ShapeDtypeStruct((t, d), BF16)],
        compiler_params=_cp("arbitrary"),
        name="mixout",
    )(mix_in, w_mix_out, x2, g_post, g1, g_pre_ffn, sc2, sh2)


def _ffn_up_kernel(h_ref, wg_ref, wu_ref, o_ref):
    h = h_ref[...]
    o_ref[...] = (_silu(_bdot(h, wg_ref[...])) * _bdot(h, wu_ref[...])).astype(o_ref.dtype)


def _ffn_up(h2, w_gate_up):
    t, d = h2.shape
    d_ff = w_gate_up.shape[1] // 2
    tm = _row_tile(t, 1024)
    tn = _row_tile(d_ff, 512)
    nj = d_ff // tn
    return pl.pallas_call(
        _ffn_up_kernel,
        grid=(t // tm, nj),
        in_specs=[pl.BlockSpec((tm, d), lambda i, j: (i, 0)),
                  pl.BlockSpec((d, tn), lambda i, j: (0, j)),
                  pl.BlockSpec((d, tn), lambda i, j: (0, nj + j))],
        out_specs=pl.BlockSpec((tm, tn), lambda i, j: (i, j)),
        out_shape=jax.ShapeDtypeStruct((t, d_ff), BF16),
        compiler_params=_cp("arbitrary", "arbitrary"),
        name="ffn_up",
    )(h2, w_gate_up, w_gate_up)


def _ffn_down_kernel(a_ref, w_ref, x1_ref, gpost_ref, g2_ref, o_ref):
    f = _bdot(a_ref[...], w_ref[...])
    nf = f * lax.rsqrt(jnp.mean(f * f, axis=-1, keepdims=True) + EPS) * gpost_ref[...]
    o_ref[...] = x1_ref[...] + g2_ref[0] * nf


def _ffn_down(act, w_down, x1, g_post, g2, seq):
    t, d_ff = act.shape
    d = w_down.shape[1]
    tm = _row_tile(seq, 512)
    per_b = seq // tm
    return pl.pallas_call(
        _ffn_down_kernel,
        grid=(t // tm,),
        in_specs=[pl.BlockSpec((tm, d_ff), lambda i: (i, 0)),
                  pl.BlockSpec((d_ff, d), lambda i: (0, 0), pipeline_mode=pl.Buffered(1)),
                  pl.BlockSpec((tm, d), lambda i: (i, 0)),
                  pl.BlockSpec((1, d), lambda i: (0, 0)),
                  pl.BlockSpec((1, 1, d), lambda i: (i // per_b, 0, 0))],
        out_specs=pl.BlockSpec((tm, d), lambda i: (i, 0)),
        out_shape=jax.ShapeDtypeStruct((t, d), F32),
        compiler_params=_cp("arbitrary"),
        name="ffn_down",
    )(act, w_down, x1, g_post, g2)


def kernel(x, c, w_ada, b_ada, g_pre_mix, g_post_mix, w_in, w_conv_ssm, b_conv_ssm, dt_bias_fwd, dt_bias_bwd, a_log_fwd, a_log_bwd, d_skip, g_ssm_norm, w_ssm_out, b_glu, w_dw, b_dw, ln_g, ln_b, w_conv_out, b_conv_out, b_gate, w_mix_out, g_pre_ffn, g_post_ffn, w_gate_up, w_down):
    bsz, seq, d = x.shape
    depth = w_ada.shape[0]
    n_heads = dt_bias_fwd.shape[1]
    d_ssm = w_ssm_out.shape[1]
    d_xbc = w_conv_ssm.shape[2]
    d_conv = w_dw.shape[2]
    pdim = d_ssm // n_heads
    assert d_xbc == d_ssm + 2 * N_GROUPS * D_STATE and seq % CHUNK == 0 and n_heads % N_GROUPS == 0
    s1, s2, s3 = d_ssm, d_ssm + d_xbc, d_ssm + d_xbc + 2 * n_heads
    glu_off, gate_off = d_ssm, d_ssm + 2 * d_conv

    x2 = x.reshape(bsz * seq, d)
    for l in range(depth):
        w_all = w_in[l].astype(BF16)
        w_zgg = jnp.concatenate([w_all[:, :s1], w_all[:, s3:]], axis=1)
        row = lambda v: v.reshape(1, -1)

        mod = _ada(c, w_ada[l], b_ada[l])
        sh1, sc1, g1, sh2, sc2, g2 = [m.reshape(bsz, 1, d) for m in jnp.split(mod, 6, axis=-1)]

        proj2, dt2 = _inproj(x2, sc1, sh1, row(g_pre_mix[l]), w_zgg, w_all, s2, s3 - s2, seq)
        proj3 = proj2.reshape(bsz, seq, -1)
        dt3 = dt2.reshape(bsz, seq, -1)

        xbc3 = _xbc(x2, sc1, sh1, row(g_pre_mix[l]), w_all, s1, w_conv_ssm[l], b_conv_ssm[l],
                    seq).reshape(bsz, seq, d_xbc)
        bias_row = row(jnp.concatenate([dt_bias_fwd[l], dt_bias_bwd[l]]))
        alog_row = row(jnp.concatenate([a_log_fwd[l], a_log_bwd[l]]))
        dskip_row = row(jnp.repeat(d_skip[l], pdim))
        y_f = _ssd(False, xbc3, dt3, bias_row, alog_row, (dskip_row,), n_heads, d_ssm)
        y_ssd = _ssd(True, xbc3, dt3, bias_row, alog_row, (y_f, proj3, row(g_ssm_norm[l])), n_heads, d_ssm)

        u = _conf(proj3, glu_off, b_glu[l], w_dw[l], b_dw[l], ln_g[l], ln_b[l])

        mix_in = _merge(y_ssd.reshape(bsz * seq, d_ssm), u.reshape(bsz * seq, d_conv),
                        w_ssm_out[l].astype(BF16), w_conv_out[l].astype(BF16), b_conv_out[l],
                        proj2, gate_off, b_gate[l])
        x1, h2 = _mixout(mix_in, w_mix_out[l].astype(BF16), x2, row(g_post_mix[l]), g1,
                         row(g_pre_ffn[l]), sc2, sh2, seq)
        act = _ffn_up(h2, w_gate_up[l].astype(BF16))
        x2 = _ffn_down(act, w_down[l].astype(BF16), x1, row(g_post_ffn[l]), g2, seq)
    return x2.reshape(bsz, seq, d)
```

```python
import functools

import jax
import jax.numpy as jnp
from jax import lax
from jax.experimental import pallas as pl
from jax.experimental.pallas import tpu as pltpu

F32 = jnp.float32
BF16 = jnp.bfloat16

EPS = 1e-6
N_GROUPS = 8
D_STATE = 128
CHUNK = 128
GROUPS_IN_FLIGHT_FORWARD = 1
GROUPS_IN_FLIGHT_REVERSE = 2
V7X_VMEM_LIMIT_BYTES = 58 * 1024 * 1024
SUBLANES = 8
LANES = 128
LOG2E = 1.4426950408889634


def _cp(*sem):
    return pltpu.CompilerParams(dimension_semantics=sem, vmem_limit_bytes=V7X_VMEM_LIMIT_BYTES)


def _silu(v):
    return v * jax.nn.sigmoid(v)


def _softplus(v):
    return jnp.maximum(v, 0.0) + jnp.log1p(jnp.exp(-jnp.abs(v)))


def _bdot(a, b):
    return jnp.dot(a, b, preferred_element_type=F32)


def _row_tile(n, want):
    t = min(n, want)
    assert n % t == 0, (n, t)
    return t


def _ada_kernel(c_ref, w_ref, b_ref, o_ref):
    ca = _silu(c_ref[...]).astype(BF16)
    o_ref[...] = _bdot(ca, w_ref[...].astype(BF16)) + b_ref[...]


def _ada(c, w, b):
    bsz, d = c.shape
    n = w.shape[1]
    rows = 16
    tn = _row_tile(n, 1024)
    cpad = jnp.zeros((rows, d), F32).at[:bsz].set(c)
    out = pl.pallas_call(
        _ada_kernel,
        grid=(n // tn,),
        in_specs=[pl.BlockSpec((rows, d), lambda j: (0, 0)),
                  pl.BlockSpec((d, tn), lambda j: (0, j)),
                  pl.BlockSpec((1, tn), lambda j: (0, j))],
        out_specs=pl.BlockSpec((rows, tn), lambda j: (0, j)),
        out_shape=jax.ShapeDtypeStruct((rows, n), F32),
        compiler_params=_cp("arbitrary"),
        name="ada",
    )(cpad, w, b.reshape(1, n))
    return out[:bsz]


def _inproj_kernel(x_ref, sc_ref, sh_ref, g_ref, w_ref, wdt_ref, o_ref, odt_ref, h_ref):
    @pl.when(pl.program_id(1) == 0)
    def _():
        x = x_ref[...]
        y = x * lax.rsqrt(jnp.mean(x * x, axis=-1, keepdims=True) + EPS) * g_ref[...]
        h = (y * (1.0 + sc_ref[0]) + sh_ref[0]).astype(BF16)
        h_ref[...] = h
        odt_ref[...] = _bdot(h, wdt_ref[...])

    o_ref[...] = _bdot(h_ref[...], w_ref[...])


def _inproj(x2, sc, sh, g, w_main, w_all, dt_off, ndt, seq):
    t, d = x2.shape
    n = w_main.shape[1]
    tm = _row_tile(seq, 1024)
    tn = _row_tile(n, 1024)
    ncol = n // tn
    assert dt_off % ndt == 0
    per_b = seq // tm
    return pl.pallas_call(
        _inproj_kernel,
        grid=(t // tm, ncol),
        in_specs=[pl.BlockSpec((tm, d), lambda i, j: (i, 0)),
                  pl.BlockSpec((1, 1, d), lambda i, j: (i // per_b, 0, 0)),
                  pl.BlockSpec((1, 1, d), lambda i, j: (i // per_b, 0, 0)),
                  pl.BlockSpec((1, d), lambda i, j: (0, 0)),
                  pl.BlockSpec((d, tn), lambda i, j: (0, j)),
                  pl.BlockSpec((d, ndt), lambda i, j: (0, dt_off // ndt))],
        out_specs=[pl.BlockSpec((tm, tn), lambda i, j: (i, j)),
                   pl.BlockSpec((tm, ndt), lambda i, j: (i, 0))],
        out_shape=[jax.ShapeDtypeStruct((t, ncol * tn), F32),
                   jax.ShapeDtypeStruct((t, ndt), F32)],
        scratch_shapes=[pltpu.VMEM((tm, d), BF16)],
        compiler_params=_cp("arbitrary", "arbitrary"),
        name="inproj",
    )(x2, sc, sh, g, w_main, w_all)


def _fill_ext(ext_ref, prev, main, nxt, halo, ts):
    s = pl.program_id(1)
    last = pl.num_programs(1) - 1
    ext_ref[0:halo, :] = jnp.where(s > 0, prev, 0.0)
    ext_ref[halo:halo + ts, :] = main
    ext_ref[halo + ts:halo + ts + halo, :] = jnp.where(s < last, nxt, 0.0)


def _dwconv_rows(tap_rows, w_ref, b_ref, rb, lanes, taps):
    acc = jnp.broadcast_to(b_ref[:, lanes], (rb, lanes.stop - lanes.start))
    for k in range(taps):
        acc = acc + tap_rows(k) * w_ref[k:k + 1, lanes]
    return acc


def _halo_specs(ts, halo, seq, width, col_block):
    per = ts // halo
    nh = seq // halo

    def prev_map(b, s, *c):
        return (b, jnp.maximum(s * per - 1, 0), col_block(*c))

    def main_map(b, s, *c):
        return (b, s, col_block(*c))

    def next_map(b, s, *c):
        return (b, jnp.minimum((s + 1) * per, nh - 1), col_block(*c))

    return [pl.BlockSpec((1, halo, width), prev_map),
            pl.BlockSpec((1, ts, width), main_map),
            pl.BlockSpec((1, halo, width), next_map)]


def _xbc_kernel(taps, halo, tm, ncol, per_b, strip, rb, kc,
                xp_ref, xm_ref, xn_ref, sc_ref, sh_ref, g_ref, w_ref, cw_ref, cb_ref,
                o_ref, h_ref, raw_a, raw_b, shift_ref):
    s = pl.program_id(0)
    ntiles = pl.num_programs(0) - 1
    cur = jnp.minimum(s, ntiles - 1)
    tn = o_ref.shape[-1]
    first = halo - (taps - 1) // 2
    shifts = sorted({(first + k) % SUBLANES for k in range(taps)} - {0})
    nrows = shift_ref.shape[1]

    @pl.when(s == 0)
    def _():
        raw_b[...] = jnp.zeros_like(raw_b)

    @pl.when((s < ntiles) & (cur % ncol == 0))
    def _():
        def modulated(x):
            y = x * lax.rsqrt(jnp.mean(x * x, axis=-1, keepdims=True) + EPS) * g_ref[...]
            return y * (1.0 + sc_ref[0]) + sh_ref[0]

        h_ref[...] = jnp.concatenate(
            [modulated(xp_ref[...]), modulated(xm_ref[...]), modulated(xn_ref[...])], axis=0).astype(BF16)

    def step(raw_w, raw_r):
        pos = (jnp.maximum(s - 1, 0) // ncol) % per_b
        raw_r[0:halo, :] = jnp.where(pos > 0, raw_r[0:halo, :], 0.0)
        raw_r[halo + tm:, :] = jnp.where(pos < per_b - 1, raw_r[halo + tm:, :], 0.0)
        pieces = []
        for c0 in range(0, tn, strip):
            lanes = slice(c0, c0 + strip)

            def copy_shifts(lanes=lanes):
                for n, q in enumerate(shifts):
                    shift_ref[n] = raw_r[q:q + nrows, lanes]

            def tap_rows(r0, k, lanes=lanes):
                m, q = divmod(first + k, SUBLANES)
                rows = slice(r0 + SUBLANES * m, r0 + SUBLANES * m + rb)
                return raw_r[rows, lanes] if q == 0 else shift_ref[shifts.index(q), rows, :]

            def conv_rows(r0, lanes=lanes, tap_rows=tap_rows):
                acc = _dwconv_rows(functools.partial(tap_rows, r0), cw_ref, cb_ref, rb, lanes, taps)
                o_ref[r0:r0 + rb, lanes] = _silu(acc)

            pieces.append(copy_shifts)
            pieces += [functools.partial(conv_rows, r0) for r0 in range(0, tm, rb)]
        nk = h_ref.shape[1] // kc
        per_chunk = -(-len(pieces) // nk)
        for c in range(nk):
            part = _bdot(h_ref[:, c * kc:(c + 1) * kc], w_ref[c * kc:(c + 1) * kc, :])
            if c == 0:
                raw_w[...] = part
            else:
                raw_w[...] += part
            for piece in pieces[c * per_chunk:(c + 1) * per_chunk]:
                piece()

    @pl.when(s % 2 == 0)
    def _():
        step(raw_a, raw_b)

    @pl.when(s % 2 == 1)
    def _():
        step(raw_b, raw_a)


def _xbc(x2, sc, sh, g, w_all, w_off, cw, cb, seq):
    t, d = x2.shape
    n = cw.shape[1]
    taps = cw.shape[0]
    halo = SUBLANES
    tm = _row_tile(seq, 1024)
    tn = _row_tile(n, 512)
    strip = _row_tile(tn, 256)
    rb = 32
    assert w_off % tn == 0
    per_b = seq // tm
    ncol = n // tn
    ntiles = (t // tm) * ncol
    per = tm // halo
    nh = t // halo

    def row_tile(s):
        return jnp.minimum(s, ntiles - 1) // ncol

    def col_tile(s):
        return jnp.minimum(s, ntiles - 1) % ncol

    def out_map(s):
        prev = jnp.maximum(s - 1, 0)
        return (prev // ncol, prev % ncol)

    return pl.pallas_call(
        functools.partial(_xbc_kernel, taps, halo, tm, ncol, per_b, strip, rb, _row_tile(d, 256)),
        grid=(ntiles + 1,),
        in_specs=[pl.BlockSpec((halo, d), lambda s: (jnp.maximum(row_tile(s) * per - 1, 0), 0)),
                  pl.BlockSpec((tm, d), lambda s: (row_tile(s), 0)),
                  pl.BlockSpec((halo, d), lambda s: (jnp.minimum((row_tile(s) + 1) * per, nh - 1), 0)),
                  pl.BlockSpec((1, 1, d), lambda s: (row_tile(s) // per_b, 0, 0)),
                  pl.BlockSpec((1, 1, d), lambda s: (row_tile(s) // per_b, 0, 0)),
                  pl.BlockSpec((1, d), lambda s: (0, 0)),
                  pl.BlockSpec((d, tn), lambda s: (0, w_off // tn + col_tile(s))),
                  pl.BlockSpec((taps, tn), lambda s: (0, out_map(s)[1])),
                  pl.BlockSpec((1, tn), lambda s: (0, out_map(s)[1]))],
        out_specs=pl.BlockSpec((tm, tn), out_map),
        out_shape=jax.ShapeDtypeStruct((t, n), F32),
        scratch_shapes=[pltpu.VMEM((tm + 2 * halo, d), BF16),
                        pltpu.VMEM((tm + 2 * halo, tn), F32),
                        pltpu.VMEM((tm + 2 * halo, tn), F32),
                        pltpu.VMEM((min(taps, SUBLANES) - 1, tm + 2 * halo - SUBLANES, strip), F32)],
        compiler_params=_cp("arbitrary"),
        name="xbc",
    )(x2, x2, x2, sc, sh, g, w_all, cw, cb.reshape(1, n))


def _ssd_kernel(reverse, n_heads, hpg, pdim, nb, *refs):
    if reverse:
        (xs_ref, b_ref, c_ref, dt_ref, bias_ref, alog_ref, yf_ref, z_ref, gn_ref,
         o_ref, h_ref, dt_t, cs_t, wend_t, cdec_t) = refs
    else:
        (xs_ref, b_ref, c_ref, dt_ref, bias_ref, alog_ref, dskip_ref,
         o_ref, h_ref, dt_t, cs_t, wend_t, cdec_t) = refs
    ln = CHUNK
    w = hpg * pdim
    hd2 = 2 * n_heads
    nt = (((1,), (1,)), ((), ()))

    @pl.when(pl.program_id(1) == 0)
    def _():
        h_ref[...] = jnp.zeros_like(h_ref)

    ri = lax.broadcasted_iota(jnp.int32, (ln, ln), 0)
    ci = lax.broadcasted_iota(jnp.int32, (ln, ln), 1)
    mask = (ci >= ri) if reverse else (ci <= ri)
    cum = mask.astype(BF16)
    edge = 0 if reverse else ln - 1

    cs_alls = []
    for bi in range(nb):
        dt_all = _softplus(dt_ref[bi] + bias_ref[...])
        a_all = dt_all * (-jnp.exp(alog_ref[...]) * LOG2E)
        a_hi = a_all.astype(BF16)
        r1 = a_all - a_hi.astype(F32)
        a_mid = r1.astype(BF16)
        a_lo = (r1 - a_mid.astype(F32)).astype(BF16)
        cs_all = _bdot(cum, a_hi) + _bdot(cum, a_mid) + _bdot(cum, a_lo)
        cs_edge = cs_all[edge:edge + 1, :]
        dt_t[bi] = dt_all.T
        cs_t[bi] = cs_all.T
        wend_t[bi] = (dt_all * jnp.exp2(cs_edge - cs_all)).T
        cdec_t[bi] = jnp.broadcast_to(jnp.exp2(cs_edge), (ln, hd2)).T
        cs_alls.append(cs_all)

    dirbase = n_heads if reverse else 0
    def group_body(g, bi):
        cs_all = cs_alls[bi]
        lanes = slice(g * w, (g + 1) * w)
        xs = xs_ref[bi, :, lanes]
        x_t = xs.T
        bgb = b_ref[bi, :, g * D_STATE:(g + 1) * D_STATE].astype(BF16)
        cgb = c_ref[bi, :, g * D_STATE:(g + 1) * D_STATE].astype(BF16)
        cb = lax.dot_general(cgb, bgb, nt, preferred_element_type=F32)
        h_in = h_ref[bi, g]
        yoff = lax.dot_general(cgb, h_in.astype(BF16), nt, preferred_element_type=F32)
        y_parts, xw_parts, cdec_parts = [], [], []
        yield
        for jp in range(hpg // 2):
            ms, xdts, cols = [], [], []
            for j in (2 * jp, 2 * jp + 1):
                c = dirbase + g * hpg + j
                rows = slice(j * pdim, (j + 1) * pdim)
                xdts.append((x_t[rows] * dt_t[bi, c:c + 1, :]).astype(BF16))
                xw_parts.append((x_t[rows] * wend_t[bi, c:c + 1, :]).astype(BF16))
                col = jnp.broadcast_to(cs_all[:, c:c + 1], (ln, ln))
                seg = col - cs_t[bi, c:c + 1, :]
                ms.append((cb * jnp.where(mask, jnp.exp2(seg), 0.0)).astype(BF16))
                cols.append(col)
                cdec_parts.append(jnp.broadcast_to(cdec_t[bi, c:c + 1, :], (pdim, D_STATE)))
            zero = jnp.zeros((pdim, ln), BF16)
            xdt_bd = jnp.concatenate([jnp.concatenate([xdts[0], zero], axis=1),
                                      jnp.concatenate([zero, xdts[1]], axis=1)], axis=0)
            yd = lax.dot_general(jnp.concatenate(ms, axis=1), xdt_bd, nt,
                                 preferred_element_type=F32)
            din = jnp.exp2(jnp.where(ci < pdim, cols[0], cols[1]))
            y_parts.append(yd + yoff[:, 2 * jp * pdim:(2 * jp + 2) * pdim] * din)
            yield
        y = jnp.concatenate(y_parts, axis=1)
        st = _bdot(jnp.concatenate(xw_parts, axis=0), bgb)
        h_ref[bi, g] = h_in * jnp.concatenate(cdec_parts, axis=0) + st
        if reverse:
            y = (y + yf_ref[bi, :, lanes]) * _silu(z_ref[bi, :, lanes])
            y = y * lax.rsqrt(jnp.mean(y * y, axis=-1, keepdims=True) + EPS) * gn_ref[:, lanes]
            o_ref[bi, :, lanes] = y.astype(o_ref.dtype)
        else:
            o_ref[bi, :, lanes] = y + dskip_ref[:, lanes] * xs

    in_flight = GROUPS_IN_FLIGHT_REVERSE if reverse else GROUPS_IN_FLIGHT_FORWARD
    for g0 in range(0, N_GROUPS, in_flight):
        running = [group_body(g, bi) for g in range(g0, g0 + in_flight) for bi in range(nb)]
        while running:
            running = [gen for gen in running if next(gen, "done") != "done"]


def _ssd(reverse, xbc3, dt3, bias_row, alog_row, extra, n_heads, d_ssm):
    bsz, seq, _ = xbc3.shape
    hd2 = 2 * n_heads
    hpg = n_heads // N_GROUPS
    pdim = d_ssm // n_heads
    w = hpg * pdim
    nc = seq // CHUNK
    ln = CHUNK
    gn = N_GROUPS * D_STATE
    assert CHUNK == D_STATE and d_ssm % gn == 0 and 2 * pdim == CHUNK and hpg % 2 == 0
    nb = 2 if bsz % 2 == 0 else 1

    def chunk(b, z):
        return (b, nc - 1 - z if reverse else z)

    in_specs = [pl.BlockSpec((nb, ln, d_ssm), lambda b, z: (*chunk(b, z), 0)),
                pl.BlockSpec((nb, ln, gn), lambda b, z: (*chunk(b, z), d_ssm // gn)),
                pl.BlockSpec((nb, ln, gn), lambda b, z: (*chunk(b, z), d_ssm // gn + 1)),
                pl.BlockSpec((nb, ln, hd2), lambda b, z: (*chunk(b, z), 0)),
                pl.BlockSpec((1, hd2), lambda b, z: (0, 0)),
                pl.BlockSpec((1, hd2), lambda b, z: (0, 0))]
    args = [xbc3, xbc3, xbc3, dt3, bias_row, alog_row]
    if reverse:
        y_f, proj3, gnorm = extra
        in_specs += [pl.BlockSpec((nb, ln, d_ssm), lambda b, z: (*chunk(b, z), 0)),
                     pl.BlockSpec((nb, ln, d_ssm), lambda b, z: (*chunk(b, z), 0)),
                     pl.BlockSpec((1, d_ssm), lambda b, z: (0, 0))]
        args += [y_f, proj3, gnorm]
        out_dtype = BF16
    else:
        (dskip,) = extra
        in_specs += [pl.BlockSpec((1, d_ssm), lambda b, z: (0, 0))]
        args += [dskip]
        out_dtype = F32
    table = pltpu.VMEM((nb, hd2, ln), F32)
    return pl.pallas_call(
        functools.partial(_ssd_kernel, reverse, n_heads, hpg, pdim, nb),
        grid=(bsz // nb, nc),
        in_specs=in_specs,
        out_specs=pl.BlockSpec((nb, ln, d_ssm), lambda b, z: (*chunk(b, z), 0)),
        out_shape=jax.ShapeDtypeStruct((bsz, seq, d_ssm), out_dtype),
        scratch_shapes=[pltpu.VMEM((nb, N_GROUPS, w, D_STATE), F32), table, table, table, table],
        compiler_params=_cp("arbitrary", "arbitrary"),
        name="ssd_bwd" if reverse else "ssd_fwd",
    )(*args)


def _conf_kernel(taps, halo, ts, lc, pa_ref, ma_ref, na_ref, pb_ref, mb_ref, nb_ref,
                 bga_ref, bgb_ref, w_ref, bdw_ref, lng_ref, lnb_ref, o_ref, ext_ref, shift_ref, conv_ref):
    d = o_ref.shape[-1]

    def glu(a, b):
        return (a + bga_ref[...]) * jax.nn.sigmoid(b + bgb_ref[...])

    _fill_ext(ext_ref, glu(pa_ref[0], pb_ref[0]), glu(ma_ref[0], mb_ref[0]),
              glu(na_ref[0], nb_ref[0]), halo, ts)
    nrows = shift_ref.shape[1]
    for q in range(SUBLANES):
        shift_ref[q] = ext_ref[q:q + nrows, :]
    first = halo - (taps - 1) // 2

    for c0 in range(0, d, lc):
        lanes = slice(c0, c0 + lc)
        nblk = ts // SUBLANES
        acc = [jnp.broadcast_to(bdw_ref[:, lanes], (SUBLANES, lc))] * nblk
        for q in range(SUBLANES):
            ms = [(first + k) // SUBLANES for k in range(taps) if (first + k) % SUBLANES == q]
            w_b = {m: jnp.broadcast_to(w_ref[SUBLANES * m + q - first:SUBLANES * m + q - first + 1, lanes],
                                       (SUBLANES, lc)) for m in ms}
            for i in range(nblk + max(ms)):
                blk = shift_ref[q, SUBLANES * i:SUBLANES * (i + 1), lanes]
                for m in ms:
                    if 0 <= i - m < nblk:
                        acc[i - m] = acc[i - m] + blk * w_b[m]
        for i in range(nblk):
            conv_ref[SUBLANES * i:SUBLANES * (i + 1), lanes] = acc[i]

    u = conv_ref[...]
    uc = u - jnp.mean(u, axis=-1, keepdims=True)
    y = uc * lax.rsqrt(jnp.mean(uc * uc, axis=-1, keepdims=True) + EPS)
    y = y * lng_ref[...] + lnb_ref[...]
    o_ref[0] = _silu(y).astype(o_ref.dtype)


def _conf(proj3, col_off, b_glu, w_dw, b_dw, ln_g, ln_b):
    bsz, seq, _ = proj3.shape
    taps, d = w_dw.shape
    halo = 16
    ts = _row_tile(seq, 256)
    lc = LANES
    ablk = col_off // d
    specs_a = _halo_specs(ts, halo, seq, d, lambda: ablk)
    specs_b = _halo_specs(ts, halo, seq, d, lambda: ablk + 1)
    row = lambda b_, s: (0, 0)
    return pl.pallas_call(
        functools.partial(_conf_kernel, taps, halo, ts, lc),
        grid=(bsz, seq // ts),
        in_specs=specs_a + specs_b + [pl.BlockSpec((1, d), row), pl.BlockSpec((1, d), row),
                                      pl.BlockSpec((taps, d), row), pl.BlockSpec((1, d), row),
                                      pl.BlockSpec((1, d), row), pl.BlockSpec((1, d), row)],
        out_specs=pl.BlockSpec((1, ts, d), lambda b_, s: (b_, s, 0)),
        out_shape=jax.ShapeDtypeStruct((bsz, seq, d), BF16),
        scratch_shapes=[pltpu.VMEM((ts + 2 * halo, d), F32),
                        pltpu.VMEM((SUBLANES, ts + 2 * halo - SUBLANES, d), F32),
                        pltpu.VMEM((ts, d), F32)],
        compiler_params=_cp("arbitrary", "arbitrary"),
        name="conf",
    )(proj3, proj3, proj3, proj3, proj3, proj3,
      b_glu[:d].reshape(1, d), b_glu[d:].reshape(1, d), w_dw, b_dw.reshape(1, d),
      ln_g.reshape(1, d), ln_b.reshape(1, d))


def _merge_kernel(ya_ref, u_ref, wa_ref, wb_ref, bb_ref, ga_ref, gb_ref, bga_ref, bgb_ref, o_ref):
    y_a = _bdot(ya_ref[...], wa_ref[...])
    y_b = _bdot(u_ref[...], wb_ref[...]) + bb_ref[...]
    g_a = jax.nn.sigmoid(ga_ref[...] + bga_ref[...])
    g_b = jax.nn.sigmoid(gb_ref[...] + bgb_ref[...])
    o_ref[...] = (g_a * y_a + g_b * y_b).astype(o_ref.dtype)


def _merge(y_ssd, u, w_ssm_out, w_conv_out, b_conv_out, proj2, gate_off, b_gate):
    t, d_ssm = y_ssd.shape
    d_conv = u.shape[1]
    d = w_ssm_out.shape[1]
    tm = _row_tile(t, 1024)
    tn = _row_tile(d, 512)
    ga = gate_off // tn
    gb = (gate_off + d) // tn
    bg = b_gate.reshape(1, 2 * d)
    return pl.pallas_call(
        _merge_kernel,
        grid=(t // tm, d // tn),
        in_specs=[pl.BlockSpec((tm, d_ssm), lambda i, j: (i, 0)),
                  pl.BlockSpec((tm, d_conv), lambda i, j: (i, 0)),
                  pl.BlockSpec((d_ssm, tn), lambda i, j: (0, j)),
                  pl.BlockSpec((d_conv, tn), lambda i, j: (0, j)),
                  pl.BlockSpec((1, tn), lambda i, j: (0, j)),
                  pl.BlockSpec((tm, tn), lambda i, j: (i, ga + j)),
                  pl.BlockSpec((tm, tn), lambda i, j: (i, gb + j)),
                  pl.BlockSpec((1, tn), lambda i, j: (0, j)),
                  pl.BlockSpec((1, tn), lambda i, j: (0, d // tn + j))],
        out_specs=pl.BlockSpec((tm, tn), lambda i, j: (i, j)),
        out_shape=jax.ShapeDtypeStruct((t, d), BF16),
        compiler_params=_cp("arbitrary", "arbitrary"),
        name="merge",
    )(y_ssd, u, w_ssm_out, w_conv_out, b_conv_out.reshape(1, d), proj2, proj2, bg, bg)


def _mixout_kernel(m_ref, w_ref, x_ref, gpost_ref, g1_ref, gpre_ref, sc2_ref, sh2_ref, x1_ref, h2_ref):
    mix = _bdot(m_ref[...], w_ref[...])
    nm = mix * lax.rsqrt(jnp.mean(mix * mix, axis=-1, keepdims=True) + EPS) * gpost_ref[...]
    x1 = x_ref[...] + g1_ref[0] * nm
    x1_ref[...] = x1
    y = x1 * lax.rsqrt(jnp.mean(x1 * x1, axis=-1, keepdims=True) + EPS) * gpre_ref[...]
    h2_ref[...] = (y * (1.0 + sc2_ref[0]) + sh2_ref[0]).astype(h2_ref.dtype)


def _mixout(mix_in, w_mix_out, x2, g_post, g1, g_pre_ffn, sc2, sh2, seq):
    t, d = x2.shape
    tm = _row_tile(seq, 512)
    per_b = seq // tm
    row = lambda i: (0, 0)
    brow = lambda i: (i // per_b, 0, 0)
    return pl.pallas_call(
        _mixout_kernel,
        grid=(t // tm,),
        in_specs=[pl.BlockSpec((tm, d), lambda i: (i, 0)),
                  pl.BlockSpec((d, d), row),
                  pl.BlockSpec((tm, d), lambda i: (i, 0)),
                  pl.BlockSpec((1, d), row),
                  pl.BlockSpec((1, 1, d), brow),
                  pl.BlockSpec((1, d), row),
                  pl.BlockSpec((1, 1, d), brow),
                  pl.BlockSpec((1, 1, d), brow)],
        out_specs=[pl.BlockSpec((tm, d), lambda i: (i, 0)),
                   pl.BlockSpec((tm, d), lambda i: (i, 0))],
        out_shape=[jax.ShapeDtypeStruct((t, d), F32), jax.ShapeDtypeStruct((t, d), BF16)],
        compiler_params=_cp("arbitrary"),
        name="mixout",
    )(mix_in, w_mix_out, x2, g_post, g1, g_pre_ffn, sc2, sh2)


def _ffn_up_kernel(h_ref, wg_ref, wu_ref, o_ref):
    h = h_ref[...]
    o_ref[...] = (_silu(_bdot(h, wg_ref[...])) * _bdot(h, wu_ref[...])).astype(o_ref.dtype)


def _ffn_up(h2, w_gate_up):
    t, d = h2.shape
    d_ff = w_gate_up.shape[1] // 2
    tm = _row_tile(t, 1024)
    tn = _row_tile(d_ff, 512)
    nj = d_ff // tn
    return pl.pallas_call(
        _ffn_up_kernel,
        grid=(t // tm, nj),
        in_specs=[pl.BlockSpec((tm, d), lambda i, j: (i, 0)),
                  pl.BlockSpec((d, tn), lambda i, j: (0, j)),
                  pl.BlockSpec((d, tn), lambda i, j: (0, nj + j))],
        out_specs=pl.BlockSpec((tm, tn), lambda i, j: (i, j)),
        out_shape=jax.ShapeDtypeStruct((t, d_ff), BF16),
        compiler_params=_cp("arbitrary", "arbitrary"),
        name="ffn_up",
    )(h2, w_gate_up, w_gate_up)


def _ffn_down_kernel(a_ref, w_ref, x1_ref, gpost_ref, g2_ref, o_ref):
    f = _bdot(a_ref[...], w_ref[...])
    nf = f * lax.rsqrt(jnp.mean(f * f, axis=-1, keepdims=True) + EPS) * gpost_ref[...]
    o_ref[...] = x1_ref[...] + g2_ref[0] * nf


def _ffn_down(act, w_down, x1, g_post, g2, seq):
    t, d_ff = act.shape
    d = w_down.shape[1]
    tm = _row_tile(seq, 512)
    per_b = seq // tm
    return pl.pallas_call(
        _ffn_down_kernel,
        grid=(t // tm,),
        in_specs=[pl.BlockSpec((tm, d_ff), lambda i: (i, 0)),
                  pl.BlockSpec((d_ff, d), lambda i: (0, 0), pipeline_mode=pl.Buffered(1)),
                  pl.BlockSpec((tm, d), lambda i: (i, 0)),
                  pl.BlockSpec((1, d), lambda i: (0, 0)),
                  pl.BlockSpec((1, 1, d), lambda i: (i // per_b, 0, 0))],
        out_specs=pl.BlockSpec((tm, d), lambda i: (i, 0)),
        out_shape=jax.ShapeDtypeStruct((t, d), F32),
        compiler_params=_cp("arbitrary"),
        name="ffn_down",
    )(act, w_down, x1, g_post, g2)


def kernel(x, c, w_ada, b_ada, g_pre_mix, g_post_mix, w_in, w_conv_ssm, b_conv_ssm, dt_bias_fwd, dt_bias_bwd, a_log_fwd, a_log_bwd, d_skip, g_ssm_norm, w_ssm_out, b_glu, w_dw, b_dw, ln_g, ln_b, w_conv_out, b_conv_out, b_gate, w_mix_out, g_pre_ffn, g_post_ffn, w_gate_up, w_down):
    bsz, seq, d = x.shape
    depth = w_ada.shape[0]
    n_heads = dt_bias_fwd.shape[1]
    d_ssm = w_ssm_out.shape[1]
    d_xbc = w_conv_ssm.shape[2]
    d_conv = w_dw.shape[2]
    pdim = d_ssm // n_heads
    assert d_xbc == d_ssm + 2 * N_GROUPS * D_STATE and seq % CHUNK == 0 and n_heads % N_GROUPS == 0
    s1, s2, s3 = d_ssm, d_ssm + d_xbc, d_ssm + d_xbc + 2 * n_heads
    glu_off, gate_off = d_ssm, d_ssm + 2 * d_conv

    x2 = x.reshape(bsz * seq, d)
    for l in range(depth):
        w_all = w_in[l].astype(BF16)
        w_zgg = jnp.concatenate([w_all[:, :s1], w_all[:, s3:]], axis=1)
        row = lambda v: v.reshape(1, -1)

        mod = _ada(c, w_ada[l], b_ada[l])
        sh1, sc1, g1, sh2, sc2, g2 = [m.reshape(bsz, 1, d) for m in jnp.split(mod, 6, axis=-1)]

        proj2, dt2 = _inproj(x2, sc1, sh1, row(g_pre_mix[l]), w_zgg, w_all, s2, s3 - s2, seq)
        proj3 = proj2.reshape(bsz, seq, -1)
        dt3 = dt2.reshape(bsz, seq, -1)

        xbc3 = _xbc(x2, sc1, sh1, row(g_pre_mix[l]), w_all, s1, w_conv_ssm[l], b_conv_ssm[l],
                    seq).reshape(bsz, seq, d_xbc)
        bias_row = row(jnp.concatenate([dt_bias_fwd[l], dt_bias_bwd[l]]))
        alog_row = row(jnp.concatenate([a_log_fwd[l], a_log_bwd[l]]))
        dskip_row = row(jnp.repeat(d_skip[l], pdim))
        y_f = _ssd(False, xbc3, dt3, bias_row, alog_row, (dskip_row,), n_heads, d_ssm)
        y_ssd = _ssd(True, xbc3, dt3, bias_row, alog_row, (y_f, proj3, row(g_ssm_norm[l])), n_heads, d_ssm)

        u = _conf(proj3, glu_off, b_glu[l], w_dw[l], b_dw[l], ln_g[l], ln_b[l])

        mix_in = _merge(y_ssd.reshape(bsz * seq, d_ssm), u.reshape(bsz * seq, d_conv),
                        w_ssm_out[l].astype(BF16), w_conv_out[l].astype(BF16), b_conv_out[l],
                        proj2, gate_off, b_gate[l])
        x1, h2 = _mixout(mix_in, w_mix_out[l].astype(BF16), x2, row(g_post_mix[l]), g1,
                         row(g_pre_ffn[l]), sc2, sh2, seq)
        act = _ffn_up(h2, w_gate_up[l].astype(BF16))
        x2 = _ffn_down(act, w_down[l].astype(BF16), x1, row(g_post_ffn[l]), g2, seq)
    return x2.reshape(bsz, seq, d)
```

```python
import functools

import jax
import jax.numpy as jnp
from jax import lax
from jax.experimental import pallas as pl
from jax.experimental.pallas import tpu as pltpu

F32 = jnp.float32
BF16 = jnp.bfloat16

EPS = 1e-6
N_GROUPS = 8
D_STATE = 128
CHUNK = 128
GROUPS_IN_FLIGHT_FORWARD = 1
GROUPS_IN_FLIGHT_REVERSE = 2
V7X_VMEM_LIMIT_BYTES = 58 * 1024 * 1024
SUBLANES = 8
LANES = 128
LOG2E = 1.4426950408889634


def _cp(*sem):
    return pltpu.CompilerParams(dimension_semantics=sem, vmem_limit_bytes=V7X_VMEM_LIMIT_BYTES)


def _silu(v):
    return v * jax.nn.sigmoid(v)


def _softplus(v):
    return jnp.maximum(v, 0.0) + jnp.log1p(jnp.exp(-jnp.abs(v)))


def _bdot(a, b):
    return jnp.dot(a, b, preferred_element_type=F32)


def _row_tile(n, want):
    t = min(n, want)
    assert n % t == 0, (n, t)
    return t


def _ada_kernel(c_ref, w_ref, b_ref, o_ref):
    ca = _silu(c_ref[...]).astype(BF16)
    o_ref[...] = _bdot(ca, w_ref[...].astype(BF16)) + b_ref[...]


def _ada(c, w, b):
    bsz, d = c.shape
    n = w.shape[1]
    rows = 16
    tn = _row_tile(n, 1024)
    cpad = jnp.zeros((rows, d), F32).at[:bsz].set(c)
    out = pl.pallas_call(
        _ada_kernel,
        grid=(n // tn,),
        in_specs=[pl.BlockSpec((rows, d), lambda j: (0, 0)),
                  pl.BlockSpec((d, tn), lambda j: (0, j)),
                  pl.BlockSpec((1, tn), lambda j: (0, j))],
        out_specs=pl.BlockSpec((rows, tn), lambda j: (0, j)),
        out_shape=jax.ShapeDtypeStruct((rows, n), F32),
        compiler_params=_cp("arbitrary"),
        name="ada",
    )(cpad, w, b.reshape(1, n))
    return out[:bsz]


def _inproj_kernel(x_ref, sc_ref, sh_ref, g_ref, w_ref, wdt_ref, o_ref, odt_ref, h_ref):
    @pl.when(pl.program_id(1) == 0)
    def _():
        x = x_ref[...]
        y = x * lax.rsqrt(jnp.mean(x * x, axis=-1, keepdims=True) + EPS) * g_ref[...]
        h = (y * (1.0 + sc_ref[0]) + sh_ref[0]).astype(BF16)
        h_ref[...] = h
        odt_ref[...] = _bdot(h, wdt_ref[...])

    o_ref[...] = _bdot(h_ref[...], w_ref[...])


def _inproj(x2, sc, sh, g, w_main, w_all, dt_off, ndt, seq):
    t, d = x2.shape
    n = w_main.shape[1]
    tm = _row_tile(seq, 1024)
    tn = _row_tile(n, 1024)
    ncol = n // tn
    assert dt_off % ndt == 0
    per_b = seq // tm
    return pl.pallas_call(
        _inproj_kernel,
        grid=(t // tm, ncol),
        in_specs=[pl.BlockSpec((tm, d), lambda i, j: (i, 0)),
                  pl.BlockSpec((1, 1, d), lambda i, j: (i // per_b, 0, 0)),
                  pl.BlockSpec((1, 1, d), lambda i, j: (i // per_b, 0, 0)),
                  pl.BlockSpec((1, d), lambda i, j: (0, 0)),
                  pl.BlockSpec((d, tn), lambda i, j: (0, j)),
                  pl.BlockSpec((d, ndt), lambda i, j: (0, dt_off // ndt))],
        out_specs=[pl.BlockSpec((tm, tn), lambda i, j: (i, j)),
                   pl.BlockSpec((tm, ndt), lambda i, j: (i, 0))],
        out_shape=[jax.ShapeDtypeStruct((t, ncol * tn), F32),
                   jax.ShapeDtypeStruct((t, ndt), F32)],
        scratch_shapes=[pltpu.VMEM((tm, d), BF16)],
        compiler_params=_cp("arbitrary", "arbitrary"),
        name="inproj",
    )(x2, sc, sh, g, w_main, w_all)


def _fill_ext(ext_ref, prev, main, nxt, halo, ts):
    s = pl.program_id(1)
    last = pl.num_programs(1) - 1
    ext_ref[0:halo, :] = jnp.where(s > 0, prev, 0.0)
    ext_ref[halo:halo + ts, :] = main
    ext_ref[halo + ts:halo + ts + halo, :] = jnp.where(s < last, nxt, 0.0)


def _dwconv_rows(tap_rows, w_ref, b_ref, rb, lanes, taps):
    acc = jnp.broadcast_to(b_ref[:, lanes], (rb, lanes.stop - lanes.start))
    for k in range(taps):
        acc = acc + tap_rows(k) * w_ref[k:k + 1, lanes]
    return acc


def _halo_specs(ts, halo, seq, width, col_block):
    per = ts // halo
    nh = seq // halo

    def prev_map(b, s, *c):
        return (b, jnp.maximum(s * per - 1, 0), col_block(*c))

    def main_map(b, s, *c):
        return (b, s, col_block(*c))

    def next_map(b, s, *c):
        return (b, jnp.minimum((s + 1) * per, nh - 1), col_block(*c))

    return [pl.BlockSpec((1, halo, width), prev_map),
            pl.BlockSpec((1, ts, width), main_map),
            pl.BlockSpec((1, halo, width), next_map)]


def _xbc_kernel(taps, halo, tm, ncol, per_b, strip, rb, kc,
                xp_ref, xm_ref, xn_ref, sc_ref, sh_ref, g_ref, w_ref, cw_ref, cb_ref,
                o_ref, h_ref, raw_a, raw_b, shift_ref):
    s = pl.program_id(0)
    ntiles = pl.num_programs(0) - 1
    cur = jnp.minimum(s, ntiles - 1)
    tn = o_ref.shape[-1]
    first = halo - (taps - 1) // 2
    shifts = sorted({(first + k) % SUBLANES for k in range(taps)} - {0})
    nrows = shift_ref.shape[1]

    @pl.when(s == 0)
    def _():
        raw_b[...] = jnp.zeros_like(raw_b)

    @pl.when((s < ntiles) & (cur % ncol == 0))
    def _():
        def modulated(x):
            y = x * lax.rsqrt(jnp.mean(x * x, axis=-1, keepdims=True) + EPS) * g_ref[...]
            return y * (1.0 + sc_ref[0]) + sh_ref[0]

        h_ref[...] = jnp.concatenate(
            [modulated(xp_ref[...]), modulated(xm_ref[...]), modulated(xn_ref[...])], axis=0).astype(BF16)

    def step(raw_w, raw_r):
        pos = (jnp.maximum(s - 1, 0) // ncol) % per_b
        raw_r[0:halo, :] = jnp.where(pos > 0, raw_r[0:halo, :], 0.0)
        raw_r[halo + tm:, :] = jnp.where(pos < per_b - 1, raw_r[halo + tm:, :], 0.0)
        pieces = []
        for c0 in range(0, tn, strip):
            lanes = slice(c0, c0 + strip)

            def copy_shifts(lanes=lanes):
                for n, q in enumerate(shifts):
                    shift_ref[n] = raw_r[q:q + nrows, lanes]

            def tap_rows(r0, k, lanes=lanes):
                m, q = divmod(first + k, SUBLANES)
                rows = slice(r0 + SUBLANES * m, r0 + SUBLANES * m + rb)
                return raw_r[rows, lanes] if q == 0 else shift_ref[shifts.index(q), rows, :]

            def conv_rows(r0, lanes=lanes, tap_rows=tap_rows):
                acc = _dwconv_rows(functools.partial(tap_rows, r0), cw_ref, cb_ref, rb, lanes, taps)
                o_ref[r0:r0 + rb, lanes] = _silu(acc)

            pieces.append(copy_shifts)
            pieces += [functools.partial(conv_rows, r0) for r0 in range(0, tm, rb)]
        nk = h_ref.shape[1] // kc
        per_chunk = -(-len(pieces) // nk)
        for c in range(nk):
            part = _bdot(h_ref[:, c * kc:(c + 1) * kc], w_ref[c * kc:(c + 1) * kc, :])
            if c == 0:
                raw_w[...] = part
            else:
                raw_w[...] += part
            for piece in pieces[c * per_chunk:(c + 1) * per_chunk]:
                piece()

    @pl.when(s % 2 == 0)
    def _():
        step(raw_a, raw_b)

    @pl.when(s % 2 == 1)
    def _():
        step(raw_b, raw_a)


def _xbc(x2, sc, sh, g, w_all, w_off, cw, cb, seq):
    t, d = x2.shape
    n = cw.shape[1]
    taps = cw.shape[0]
    halo = SUBLANES
    tm = _row_tile(seq, 512)
    tn = _row_tile(n, 1024)
    strip = _row_tile(tn, 256)
    rb = 32
    assert w_off % tn == 0
    per_b = seq // tm
    ncol = n // tn
    ntiles = (t // tm) * ncol
    per = tm // halo
    nh = t // halo

    def row_tile(s):
        return jnp.minimum(s, ntiles - 1) // ncol

    def col_tile(s):
        return jnp.minimum(s, ntiles - 1) % ncol

    def out_map(s):
        prev = jnp.maximum(s - 1, 0)
        return (prev // ncol, prev % ncol)

    return pl.pallas_call(
        functools.partial(_xbc_kernel, taps, halo, tm, ncol, per_b, strip, rb, _row_tile(d, 256)),
        grid=(ntiles + 1,),
        in_specs=[pl.BlockSpec((halo, d), lambda s: (jnp.maximum(row_tile(s) * per - 1, 0), 0)),
                  pl.BlockSpec((tm, d), lambda s: (row_tile(s), 0)),
                  pl.BlockSpec((halo, d), lambda s: (jnp.minimum((row_tile(s) + 1) * per, nh - 1), 0)),
                  pl.BlockSpec((1, 1, d), lambda s: (row_tile(s) // per_b, 0, 0)),
                  pl.BlockSpec((1, 1, d), lambda s: (row_tile(s) // per_b, 0, 0)),
                  pl.BlockSpec((1, d), lambda s: (0, 0)),
                  pl.BlockSpec((d, tn), lambda s: (0, w_off // tn + col_tile(s))),
                  pl.BlockSpec((taps, tn), lambda s: (0, out_map(s)[1])),
                  pl.BlockSpec((1, tn), lambda s: (0, out_map(s)[1]))],
        out_specs=pl.BlockSpec((tm, tn), out_map),
        out_shape=jax.ShapeDtypeStruct((t, n), F32),
        scratch_shapes=[pltpu.VMEM((tm + 2 * halo, d), BF16),
                        pltpu.VMEM((tm + 2 * halo, tn), F32),
                        pltpu.VMEM((tm + 2 * halo, tn), F32),
                        pltpu.VMEM((min(taps, SUBLANES) - 1, tm + 2 * halo - SUBLANES, strip), F32)],
        compiler_params=_cp("arbitrary"),
        name="xbc",
    )(x2, x2, x2, sc, sh, g, w_all, cw, cb.reshape(1, n))


def _ssd_kernel(reverse, n_heads, hpg, pdim, nb, *refs):
    if reverse:
        (xs_ref, b_ref, c_ref, dt_ref, bias_ref, alog_ref, yf_ref, z_ref, gn_ref,
         o_ref, h_ref, dt_t, cs_t, wend_t, cdec_t) = refs
    else:
        (xs_ref, b_ref, c_ref, dt_ref, bias_ref, alog_ref, dskip_ref,
         o_ref, h_ref, dt_t, cs_t, wend_t, cdec_t) = refs
    ln = CHUNK
    w = hpg * pdim
    hd2 = 2 * n_heads
    nt = (((1,), (1,)), ((), ()))

    @pl.when(pl.program_id(1) == 0)
    def _():
        h_ref[...] = jnp.zeros_like(h_ref)

    ri = lax.broadcasted_iota(jnp.int32, (ln, ln), 0)
    ci = lax.broadcasted_iota(jnp.int32, (ln, ln), 1)
    mask = (ci >= ri) if reverse else (ci <= ri)
    cum = mask.astype(BF16)
    edge = 0 if reverse else ln - 1

    cs_alls = []
    for bi in range(nb):
        dt_all = _softplus(dt_ref[bi] + bias_ref[...])
        a_all = dt_all * (-jnp.exp(alog_ref[...]) * LOG2E)
        a_hi = a_all.astype(BF16)
        r1 = a_all - a_hi.astype(F32)
        a_mid = r1.astype(BF16)
        a_lo = (r1 - a_mid.astype(F32)).astype(BF16)
        cs_all = _bdot(cum, a_hi) + _bdot(cum, a_mid) + _bdot(cum, a_lo)
        cs_edge = cs_all[edge:edge + 1, :]
        dt_t[bi] = dt_all.T
        cs_t[bi] = cs_all.T
        wend_t[bi] = (dt_all * jnp.exp2(cs_edge - cs_all)).T
        cdec_t[bi] = jnp.broadcast_to(jnp.exp2(cs_edge), (ln, hd2)).T
        cs_alls.append(cs_all)

    dirbase = n_heads if reverse else 0
    def group_body(g, bi):
        cs_all = cs_alls[bi]
        lanes = slice(g * w, (g + 1) * w)
        xs = xs_ref[bi, :, lanes]
        x_t = xs.T
        bgb = b_ref[bi, :, g * D_STATE:(g + 1) * D_STATE].astype(BF16)
        cgb = c_ref[bi, :, g * D_STATE:(g + 1) * D_STATE].astype(BF16)
        cb = lax.dot_general(cgb, bgb, nt, preferred_element_type=F32)
        h_in = h_ref[bi, g]
        yoff = lax.dot_general(cgb, h_in.astype(BF16), nt, preferred_element_type=F32)
        y_parts, xw_parts, cdec_parts = [], [], []
        yield
        for jp in range(hpg // 2):
            ms, xdts, cols = [], [], []
            for j in (2 * jp, 2 * jp + 1):
                c = dirbase + g * hpg + j
                rows = slice(j * pdim, (j + 1) * pdim)
                xdts.append((x_t[rows] * dt_t[bi, c:c + 1, :]).astype(BF16))
                xw_parts.append((x_t[rows] * wend_t[bi, c:c + 1, :]).astype(BF16))
                col = jnp.broadcast_to(cs_all[:, c:c + 1], (ln, ln))
                seg = col - cs_t[bi, c:c + 1, :]
                ms.append((cb * jnp.where(mask, jnp.exp2(seg), 0.0)).astype(BF16))
                cols.append(col)
                cdec_parts.append(jnp.broadcast_to(cdec_t[bi, c:c + 1, :], (pdim, D_STATE)))
            zero = jnp.zeros((pdim, ln), BF16)
            xdt_bd = jnp.concatenate([jnp.concatenate([xdts[0], zero], axis=1),
                                      jnp.concatenate([zero, xdts[1]], axis=1)], axis=0)
            yd = lax.dot_general(jnp.concatenate(ms, axis=1), xdt_bd, nt,
                                 preferred_element_type=F32)
            din = jnp.exp2(jnp.where(ci < pdim, cols[0], cols[1]))
            y_parts.append(yd + yoff[:, 2 * jp * pdim:(2 * jp + 2) * pdim] * din)
            yield
        y = jnp.concatenate(y_parts, axis=1)
        st = _bdot(jnp.concatenate(xw_parts, axis=0), bgb)
        h_ref[bi, g] = h_in * jnp.concatenate(cdec_parts, axis=0) + st
        if reverse:
            y = (y + yf_ref[bi, :, lanes]) * _silu(z_ref[bi, :, lanes])
            y = y * lax.rsqrt(jnp.mean(y * y, axis=-1, keepdims=True) + EPS) * gn_ref[:, lanes]
            o_ref[bi, :, lanes] = y.astype(o_ref.dtype)
        else:
            o_ref[bi, :, lanes] = y + dskip_ref[:, lanes] * xs

    in_flight = GROUPS_IN_FLIGHT_REVERSE if reverse else GROUPS_IN_FLIGHT_FORWARD
    for g0 in range(0, N_GROUPS, in_flight):
        running = [group_body(g, bi) for g in range(g0, g0 + in_flight) for bi in range(nb)]
        while running:
            running = [gen for gen in running if next(gen, "done") != "done"]


def _ssd(reverse, xbc3, dt3, bias_row, alog_row, extra, n_heads, d_ssm):
    bsz, seq, _ = xbc3.shape
    hd2 = 2 * n_heads
    hpg = n_heads // N_GROUPS
    pdim = d_ssm // n_heads
    w = hpg * pdim
    nc = seq // CHUNK
    ln = CHUNK
    gn = N_GROUPS * D_STATE
    assert CHUNK == D_STATE and d_ssm % gn == 0 and 2 * pdim == CHUNK and hpg % 2 == 0
    nb = 2 if bsz % 2 == 0 else 1

    def chunk(b, z):
        return (b, nc - 1 - z if reverse else z)

    in_specs = [pl.BlockSpec((nb, ln, d_ssm), lambda b, z: (*chunk(b, z), 0)),
                pl.BlockSpec((nb, ln, gn), lambda b, z: (*chunk(b, z), d_ssm // gn)),
                pl.BlockSpec((nb, ln, gn), lambda b, z: (*chunk(b, z), d_ssm // gn + 1)),
                pl.BlockSpec((nb, ln, hd2), lambda b, z: (*chunk(b, z), 0)),
                pl.BlockSpec((1, hd2), lambda b, z: (0, 0)),
                pl.BlockSpec((1, hd2), lambda b, z: (0, 0))]
    args = [xbc3, xbc3, xbc3, dt3, bias_row, alog_row]
    if reverse:
        y_f, proj3, gnorm = extra
        in_specs += [pl.BlockSpec((nb, ln, d_ssm), lambda b, z: (*chunk(b, z), 0)),
                     pl.BlockSpec((nb, ln, d_ssm), lambda b, z: (*chunk(b, z), 0)),
                     pl.BlockSpec((1, d_ssm), lambda b, z: (0, 0))]
        args += [y_f, proj3, gnorm]
        out_dtype = BF16
    else:
        (dskip,) = extra
        in_specs += [pl.BlockSpec((1, d_ssm), lambda b, z: (0, 0))]
        args += [dskip]
        out_dtype = F32
    table = pltpu.VMEM((nb, hd2, ln), F32)
    return pl.pallas_call(
        functools.partial(_ssd_kernel, reverse, n_heads, hpg, pdim, nb),
        grid=(bsz // nb, nc),
        in_specs=in_specs,
        out_specs=pl.BlockSpec((nb, ln, d_ssm), lambda b, z: (*chunk(b, z), 0)),
        out_shape=jax.ShapeDtypeStruct((bsz, seq, d_ssm), out_dtype),
        scratch_shapes=[pltpu.VMEM((nb, N_GROUPS, w, D_STATE), F32), table, table, table, table],
        compiler_params=_cp("arbitrary", "arbitrary"),
        name="ssd_bwd" if reverse else "ssd_fwd",
    )(*args)


def _conf_kernel(taps, halo, ts, lc, pa_ref, ma_ref, na_ref, pb_ref, mb_ref, nb_ref,
                 bga_ref, bgb_ref, w_ref, bdw_ref, lng_ref, lnb_ref, o_ref, ext_ref, shift_ref, conv_ref):
    d = o_ref.shape[-1]

    def glu(a, b):
        return (a + bga_ref[...]) * jax.nn.sigmoid(b + bgb_ref[...])

    _fill_ext(ext_ref, glu(pa_ref[0], pb_ref[0]), glu(ma_ref[0], mb_ref[0]),
              glu(na_ref[0], nb_ref[0]), halo, ts)
    nrows = shift_ref.shape[1]
    for q in range(SUBLANES):
        shift_ref[q] = ext_ref[q:q + nrows, :]
    first = halo - (taps - 1) // 2

    for c0 in range(0, d, lc):
        lanes = slice(c0, c0 + lc)
        nblk = ts // SUBLANES
        acc = [jnp.broadcast_to(bdw_ref[:, lanes], (SUBLANES, lc))] * nblk
        for q in range(SUBLANES):
            ms = [(first + k) // SUBLANES for k in range(taps) if (first + k) % SUBLANES == q]
            w_b = {m: jnp.broadcast_to(w_ref[SUBLANES * m + q - first:SUBLANES * m + q - first + 1, lanes],
                                       (SUBLANES, lc)) for m in ms}
            for i in range(nblk + max(ms)):
                blk = shift_ref[q, SUBLANES * i:SUBLANES * (i + 1), lanes]
                for m in ms:
                    if 0 <= i - m < nblk:
                        acc[i - m] = acc[i - m] + blk * w_b[m]
        for i in range(nblk):
            conv_ref[SUBLANES * i:SUBLANES * (i + 1), lanes] = acc[i]

    u = conv_ref[...]
    uc = u - jnp.mean(u, axis=-1, keepdims=True)
    y = uc * lax.rsqrt(jnp.mean(uc * uc, axis=-1, keepdims=True) + EPS)
    y = y * lng_ref[...] + lnb_ref[...]
    o_ref[0] = _silu(y).astype(o_ref.dtype)


def _conf(proj3, col_off, b_glu, w_dw, b_dw, ln_g, ln_b):
    bsz, seq, _ = proj3.shape
    taps, d = w_dw.shape
    halo = 16
    ts = _row_tile(seq, 256)
    lc = LANES
    ablk = col_off // d
    specs_a = _halo_specs(ts, halo, seq, d, lambda: ablk)
    specs_b = _halo_specs(ts, halo, seq, d, lambda: ablk + 1)
    row = lambda b_, s: (0, 0)
    return pl.pallas_call(
        functools.partial(_conf_kernel, taps, halo, ts, lc),
        grid=(bsz, seq // ts),
        in_specs=specs_a + specs_b + [pl.BlockSpec((1, d), row), pl.BlockSpec((1, d), row),
                                      pl.BlockSpec((taps, d), row), pl.BlockSpec((1, d), row),
                                      pl.BlockSpec((1, d), row), pl.BlockSpec((1, d), row)],
        out_specs=pl.BlockSpec((1, ts, d), lambda b_, s: (b_, s, 0)),
        out_shape=jax.ShapeDtypeStruct((bsz, seq, d), BF16),
        scratch_shapes=[pltpu.VMEM((ts + 2 * halo, d), F32),
                        pltpu.VMEM((SUBLANES, ts + 2 * halo - SUBLANES, d), F32),
                        pltpu.VMEM((ts, d), F32)],
        compiler_params=_cp("arbitrary", "arbitrary"),
        name="conf",
    )(proj3, proj3, proj3, proj3, proj3, proj3,
      b_glu[:d].reshape(1, d), b_glu[d:].reshape(1, d), w_dw, b_dw.reshape(1, d),
      ln_g.reshape(1, d), ln_b.reshape(1, d))


def _merge_kernel(ya_ref, u_ref, wa_ref, wb_ref, bb_ref, ga_ref, gb_ref, bga_ref, bgb_ref, o_ref):
    y_a = _bdot(ya_ref[...], wa_ref[...])
    y_b = _bdot(u_ref[...], wb_ref[...]) + bb_ref[...]
    g_a = jax.nn.sigmoid(ga_ref[...] + bga_ref[...])
    g_b = jax.nn.sigmoid(gb_ref[...] + bgb_ref[...])
    o_ref[...] = (g_a * y_a + g_b * y_b).astype(o_ref.dtype)


def _merge(y_ssd, u, w_ssm_out, w_conv_out, b_conv_out, proj2, gate_off, b_gate):
    t, d_ssm = y_ssd.shape
    d_conv = u.shape[1]
    d = w_ssm_out.shape[1]
    tm = _row_tile(t, 1024)
    tn = _row_tile(d, 512)
    ga = gate_off // tn
    gb = (gate_off + d) // tn
    bg = b_gate.reshape(1, 2 * d)
    return pl.pallas_call(
        _merge_kernel,
        grid=(t // tm, d // tn),
        in_specs=[pl.BlockSpec((tm, d_ssm), lambda i, j: (i, 0)),
                  pl.BlockSpec((tm, d_conv), lambda i, j: (i, 0)),
                  pl.BlockSpec((d_ssm, tn), lambda i, j: (0, j)),
                  pl.BlockSpec((d_conv, tn), lambda i, j: (0, j)),
                  pl.BlockSpec((1, tn), lambda i, j: (0, j)),
                  pl.BlockSpec((tm, tn), lambda i, j: (i, ga + j)),
                  pl.BlockSpec((tm, tn), lambda i, j: (i, gb + j)),
                  pl.BlockSpec((1, tn), lambda i, j: (0, j)),
                  pl.BlockSpec((1, tn), lambda i, j: (0, d // tn + j))],
        out_specs=pl.BlockSpec((tm, tn), lambda i, j: (i, j)),
        out_shape=jax.ShapeDtypeStruct((t, d), BF16),
        compiler_params=_cp("arbitrary", "arbitrary"),
        name="merge",
    )(y_ssd, u, w_ssm_out, w_conv_out, b_conv_out.reshape(1, d), proj2, proj2, bg, bg)


def _mixout_kernel(m_ref, w_ref, x_ref, gpost_ref, g1_ref, gpre_ref, sc2_ref, sh2_ref, x1_ref, h2_ref):
    mix = _bdot(m_ref[...], w_ref[...])
    nm = mix * lax.rsqrt(jnp.mean(mix * mix, axis=-1, keepdims=True) + EPS) * gpost_ref[...]
    x1 = x_ref[...] + g1_ref[0] * nm
    x1_ref[...] = x1
    y = x1 * lax.rsqrt(jnp.mean(x1 * x1, axis=-1, keepdims=True) + EPS) * gpre_ref[...]
    h2_ref[...] = (y * (1.0 + sc2_ref[0]) + sh2_ref[0]).astype(h2_ref.dtype)


def _mixout(mix_in, w_mix_out, x2, g_post, g1, g_pre_ffn, sc2, sh2, seq):
    t, d = x2.shape
    tm = _row_tile(seq, 512)
    per_b = seq // tm
    row = lambda i: (0, 0)
    brow = lambda i: (i // per_b, 0, 0)
    return pl.pallas_call(
        _mixout_kernel,
        grid=(t // tm,),
        in_specs=[pl.BlockSpec((tm, d), lambda i: (i, 0)),
                  pl.BlockSpec((d, d), row),
                  pl.BlockSpec((tm, d), lambda i: (i, 0)),
                  pl.BlockSpec((1, d), row),
                  pl.BlockSpec((1, 1, d), brow),
                  pl.BlockSpec((1, d), row),
                  pl.BlockSpec((1, 1, d), brow),
                  pl.BlockSpec((1, 1, d), brow)],
        out_specs=[pl.BlockSpec((tm, d), lambda i: (i, 0)),
                   pl.BlockSpec((tm, d), lambda i: (i, 0))],
        out_shape=[jax.ShapeDtypeStruct((t, d), F32), jax.ShapeDtypeStruct((t, d), BF16)],
        compiler_params=_cp("arbitrary"),
        name="mixout",
    )(mix_in, w_mix_out, x2, g_post, g1, g_pre_ffn, sc2, sh2)


def _ffn_up_kernel(h_ref, wg_ref, wu_ref, o_ref):
    h = h_ref[...]
    o_ref[...] = (_silu(_bdot(h, wg_ref[...])) * _bdot(h, wu_ref[...])).astype(o_ref.dtype)


def _ffn_up(h2, w_gate_up):
    t, d = h2.shape
    d_ff = w_gate_up.shape[1] // 2
    tm = _row_tile(t, 1024)
    tn = _row_tile(d_ff, 512)
    nj = d_ff // tn
    return pl.pallas_call(
        _ffn_up_kernel,
        grid=(t // tm, nj),
        in_specs=[pl.BlockSpec((tm, d), lambda i, j: (i, 0)),
                  pl.BlockSpec((d, tn), lambda i, j: (0, j)),
                  pl.BlockSpec((d, tn), lambda i, j: (0, nj + j))],
        out_specs=pl.BlockSpec((tm, tn), lambda i, j: (i, j)),
        out_shape=jax.ShapeDtypeStruct((t, d_ff), BF16),
        compiler_params=_cp("arbitrary", "arbitrary"),
        name="ffn_up",
    )(h2, w_gate_up, w_gate_up)


def _ffn_down_kernel(a_ref, w_ref, x1_ref, gpost_ref, g2_ref, o_ref):
    f = _bdot(a_ref[...], w_ref[...])
    nf = f * lax.rsqrt(jnp.mean(f * f, axis=-1, keepdims=True) + EPS) * gpost_ref[...]
    o_ref[...] = x1_ref[...] + g2_ref[0] * nf


def _ffn_down(act, w_down, x1, g_post, g2, seq):
    t, d_ff = act.shape
    d = w_down.shape[1]
    tm = _row_tile(seq, 512)
    per_b = seq // tm
    return pl.pallas_call(
        _ffn_down_kernel,
        grid=(t // tm,),
        in_specs=[pl.BlockSpec((tm, d_ff), lambda i: (i, 0)),
                  pl.BlockSpec((d_ff, d), lambda i: (0, 0), pipeline_mode=pl.Buffered(1)),
                  pl.BlockSpec((tm, d), lambda i: (i, 0)),
                  pl.BlockSpec((1, d), lambda i: (0, 0)),
                  pl.BlockSpec((1, 1, d), lambda i: (i // per_b, 0, 0))],
        out_specs=pl.BlockSpec((tm, d), lambda i: (i, 0)),
        out_shape=jax.ShapeDtypeStruct((t, d), F32),
        compiler_params=_cp("arbitrary"),
        name="ffn_down",
    )(act, w_down, x1, g_post, g2)


def kernel(x, c, w_ada, b_ada, g_pre_mix, g_post_mix, w_in, w_conv_ssm, b_conv_ssm, dt_bias_fwd, dt_bias_bwd, a_log_fwd, a_log_bwd, d_skip, g_ssm_norm, w_ssm_out, b_glu, w_dw, b_dw, ln_g, ln_b, w_conv_out, b_conv_out, b_gate, w_mix_out, g_pre_ffn, g_post_ffn, w_gate_up, w_down):
    bsz, seq, d = x.shape
    depth = w_ada.shape[0]
    n_heads = dt_bias_fwd.shape[1]
    d_ssm = w_ssm_out.shape[1]
    d_xbc = w_conv_ssm.shape[2]
    d_conv = w_dw.shape[2]
    pdim = d_ssm // n_heads
    assert d_xbc == d_ssm + 2 * N_GROUPS * D_STATE and seq % CHUNK == 0 and n_heads % N_GROUPS == 0
    s1, s2, s3 = d_ssm, d_ssm + d_xbc, d_ssm + d_xbc + 2 * n_heads
    glu_off, gate_off = d_ssm, d_ssm + 2 * d_conv

    x2 = x.reshape(bsz * seq, d)
    for l in range(depth):
        w_all = w_in[l].astype(BF16)
        w_zgg = jnp.concatenate([w_all[:, :s1], w_all[:, s3:]], axis=1)
        row = lambda v: v.reshape(1, -1)

        mod = _ada(c, w_ada[l], b_ada[l])
        sh1, sc1, g1, sh2, sc2, g2 = [m.reshape(bsz, 1, d) for m in jnp.split(mod, 6, axis=-1)]

        proj2, dt2 = _inproj(x2, sc1, sh1, row(g_pre_mix[l]), w_zgg, w_all, s2, s3 - s2, seq)
        proj3 = proj2.reshape(bsz, seq, -1)
        dt3 = dt2.reshape(bsz, seq, -1)

        xbc3 = _xbc(x2, sc1, sh1, row(g_pre_mix[l]), w_all, s1, w_conv_ssm[l], b_conv_ssm[l],
                    seq).reshape(bsz, seq, d_xbc)
        bias_row = row(jnp.concatenate([dt_bias_fwd[l], dt_bias_bwd[l]]))
        alog_row = row(jnp.concatenate([a_log_fwd[l], a_log_bwd[l]]))
        dskip_row = row(jnp.repeat(d_skip[l], pdim))
        y_f = _ssd(False, xbc3, dt3, bias_row, alog_row, (dskip_row,), n_heads, d_ssm)
        y_ssd = _ssd(True, xbc3, dt3, bias_row, alog_row, (y_f, proj3, row(g_ssm_norm[l])), n_heads, d_ssm)

        u = _conf(proj3, glu_off, b_glu[l], w_dw[l], b_dw[l], ln_g[l], ln_b[l])

        mix_in = _merge(y_ssd.reshape(bsz * seq, d_ssm), u.reshape(bsz * seq, d_conv),
                        w_ssm_out[l].astype(BF16), w_conv_out[l].astype(BF16), b_conv_out[l],
                        proj2, gate_off, b_gate[l])
        x1, h2 = _mixout(mix_in, w_mix_out[l].astype(BF16), x2, row(g_post_mix[l]), g1,
                         row(g_pre_ffn[l]), sc2, sh2, seq)
        act = _ffn_up(h2, w_gate_up[l].astype(BF16))
        x2 = _ffn_down(act, w_down[l].astype(BF16), x1, row(g_post_ffn[l]), g2, seq)
    return x2.reshape(bsz, seq, d)
```
